```python
import jax, jax.numpy as jnp
from jax import lax
import numpy as np

D_MODEL = 1024
BATCH = 2
SEQ = 16384
DEPTH = 2

CHUNK = 64
N_EVEN = (DEPTH + 1) // 2
N_ODD = DEPTH // 2
EPS = 1e-6

CONV_CH = D_MODEL // 2
CONV_W = 3
FOX_HEADS = 8
FOX_HEAD_DIM = 64
FOX_DIM = FOX_HEADS * FOX_HEAD_DIM
Q_BLOCK = 128
EVEN_IN = 3 * CONV_CH + 3 * FOX_DIM + FOX_HEADS
EVEN_SPLITS = (CONV_CH, 2 * CONV_CH, 3 * CONV_CH, 3 * CONV_CH + FOX_DIM,
               3 * CONV_CH + 2 * FOX_DIM, 3 * CONV_CH + 3 * FOX_DIM)

LRU_WIDTH = D_MODEL
LRU_HEADS = 8
LRU_BLOCK = LRU_WIDTH // LRU_HEADS
LRU_CONV_W = 4
LRU_C = 8.0

PEER_HEADS = 8
N_KEYS = 128
N_EXPERTS = N_KEYS * N_KEYS
PEER_TOPK = 16
QUERY_DIM = 256
HALF = QUERY_DIM // 2
TOKEN_BLOCK = 128

PLE_DIM = 256

kernel_name = "hybrid_conv_fox_rglru_peer_trunk"


def rms_norm(x, g):
    xf = x.astype(jnp.float32)
    y = xf * lax.rsqrt(jnp.mean(xf * xf, axis=-1, keepdims=True) + EPS)
    return (y * g.astype(jnp.float32)).astype(x.dtype)


def causal_conv(x, w):
    width = w.shape[0]
    s = x.shape[1]
    xp = jnp.pad(x, ((0, 0), (width - 1, 0), (0, 0)))
    y = xp[:, 0:s, :] * w[0]
    for k in range(1, width):
        y = y + xp[:, k:k + s, :] * w[k]
    return y


def forgetting_attention(q, k, v, log_f):
    b, s, h, dh = q.shape
    scale = dh ** -0.5
    fh = jnp.cumsum(log_f, axis=1).transpose(0, 2, 1)
    qh = q.transpose(0, 2, 1, 3)
    kh = k.transpose(0, 2, 1, 3)
    vh = v.transpose(0, 2, 1, 3)
    n_blk = s // Q_BLOCK

    def q_block(qi):
        start = qi * Q_BLOCK
        qb = lax.dynamic_slice_in_dim(qh, start, Q_BLOCK, axis=2).astype(jnp.float32)
        fq = lax.dynamic_slice_in_dim(fh, start, Q_BLOCK, axis=2)
        tq = start + jnp.arange(Q_BLOCK)

        def kv_step(kj, carry):
            m, l, acc = carry
            ks = kj * Q_BLOCK
            kb = lax.dynamic_slice_in_dim(kh, ks, Q_BLOCK, axis=2).astype(jnp.float32)
            vb = lax.dynamic_slice_in_dim(vh, ks, Q_BLOCK, axis=2).astype(jnp.float32)
            fk = lax.dynamic_slice_in_dim(fh, ks, Q_BLOCK, axis=2)
            tk = ks + jnp.arange(Q_BLOCK)
            sc = (jnp.einsum('bhqd,bhkd->bhqk', qb, kb) * scale
                  + fq[..., :, None] - fk[..., None, :])
            sc = jnp.where(tk[None, :] <= tq[:, None], sc, -jnp.inf)
            m_new = jnp.maximum(m, jnp.max(sc, axis=-1))
            pr = jnp.exp(sc - m_new[..., None])
            corr = jnp.exp(m - m_new)
            l = l * corr + jnp.sum(pr, axis=-1)
            acc = acc * corr[..., None] + jnp.einsum('bhqk,bhkd->bhqd', pr, vb)
            return m_new, l, acc

        init = (jnp.full((b, h, Q_BLOCK), -jnp.inf, jnp.float32),
                jnp.zeros((b, h, Q_BLOCK), jnp.float32),
                jnp.zeros((b, h, Q_BLOCK, dh), jnp.float32))
        _, l, acc = lax.fori_loop(0, qi + 1, kv_step, init)
        return acc / l[..., None]

    out = lax.map(q_block, jnp.arange(n_blk))
    out = out.transpose(1, 0, 3, 2, 4).reshape(b, s, h * dh)
    return out.astype(q.dtype)


def mix_even(hn, w_in, b_forget, conv_w, q_gain, k_gain, w_out):
    b, s, _ = hn.shape
    z = hn @ w_in
    gb, gc, xa, q, k, v, f = jnp.split(z, EVEN_SPLITS, axis=-1)
    ya = gb * causal_conv(gc * xa, conv_w)
    q = rms_norm(q.reshape(b, s, FOX_HEADS, FOX_HEAD_DIM), q_gain)
    k = rms_norm(k.reshape(b, s, FOX_HEADS, FOX_HEAD_DIM), k_gain)
    v = v.reshape(b, s, FOX_HEADS, FOX_HEAD_DIM)
    log_f = jax.nn.log_sigmoid(f.astype(jnp.float32) + b_forget.astype(jnp.float32))
    yb = forgetting_attention(q, k, v, log_f)
    return jnp.concatenate([ya, yb], axis=-1) @ w_out


def _lin_combine(c1, c2):
    a1, b1 = c1
    a2, b2 = c2
    return a1 * a2, a2 * b1 + b2


def mix_odd(hn, w_in, conv_w, conv_b, w_a, b_a, w_x, b_x, lru_param, w_out):
    b, s, _ = hn.shape
    gate, xr = jnp.split(hn @ w_in, 2, axis=-1)
    xr = causal_conv(xr, conv_w) + conv_b
    xb = xr.reshape(b, s, LRU_HEADS, LRU_BLOCK)
    r = jax.nn.sigmoid(jnp.einsum('bshi,hij->bshj', xb, w_a).reshape(b, s, LRU_WIDTH)
                       .astype(jnp.float32) + b_a.astype(jnp.float32))
    i = jax.nn.sigmoid(jnp.einsum('bshi,hij->bshj', xb, w_x).reshape(b, s, LRU_WIDTH)
                       .astype(jnp.float32) + b_x.astype(jnp.float32))
    log_a = -LRU_C * r * jax.nn.softplus(-lru_param.astype(jnp.float32))
    a = jnp.exp(log_a)
    mult = jnp.sqrt(-jnp.expm1(2.0 * log_a))
    u = mult * (i * xr.astype(jnp.float32))
    _, hs = lax.associative_scan(_lin_combine, (a, u), axis=1)
    y = hs.astype(hn.dtype) * jax.nn.gelu(gate)
    return y @ w_out


def peer(hn, w_query, sub_keys, expert_u, expert_v):
    b, s, d = hn.shape
    tokens = hn.reshape(-1, TOKEN_BLOCK, d)

    def block(xt):
        t = xt.shape[0]
        q = (xt @ w_query).reshape(t, PEER_HEADS, 2, HALF)
        sc = jnp.einsum('thcd,ckd->thck', q, sub_keys).astype(jnp.float32)
        top_s, top_i = lax.top_k(sc, PEER_TOPK)
        cand = top_s[:, :, 0, :, None] + top_s[:, :, 1, None, :]
        best_s, best_c = lax.top_k(cand.reshape(t, PEER_HEADS, PEER_TOPK * PEER_TOPK), PEER_TOPK)
        i1 = jnp.take_along_axis(top_i[:, :, 0], best_c // PEER_TOPK, axis=-1)
        i2 = jnp.take_along_axis(top_i[:, :, 1], best_c % PEER_TOPK, axis=-1)
        idx = (i1 * N_KEYS + i2).reshape(t, PEER_HEADS * PEER_TOPK)
        g = jax.nn.softmax(best_s, axis=-1).reshape(t, PEER_HEADS * PEER_TOPK)
        u = expert_u[idx]
        v = expert_v[idx]
        act = jax.nn.gelu(jnp.einsum('td,tnd->tn', xt, u).astype(jnp.float32))
        return jnp.einsum('tn,tnd->td', (g * act).astype(xt.dtype), v)

    out = lax.map(block, tokens)
    return out.reshape(b, s, d)


def setup_inputs(seed: int = 0) -> dict:
    key = jax.random.key(seed)
    ks = jax.random.split(key, 32)

    def nrm(k, shape, fan_in):
        return jax.random.normal(k, shape, jnp.float32) * (fan_in ** -0.5)

    def gain(k, shape):
        return 1.0 + 0.05 * jax.random.normal(k, shape, jnp.float32)

    def small(k, shape):
        return 0.1 * jax.random.normal(k, shape, jnp.float32)

    rad = jax.random.uniform(ks[19], (N_ODD, LRU_WIDTH), jnp.float32, minval=0.9, maxval=0.999)
    base = rad ** (1.0 / LRU_C)
    lru_param = jnp.log(base) - jnp.log1p(-base)

    return {
        "x": jax.random.normal(ks[0], (BATCH, SEQ, D_MODEL), jnp.float32),
        "p": jax.random.normal(ks[1], (DEPTH, BATCH, SEQ, PLE_DIM), jnp.float32),
        "norm_mix": gain(ks[2], (DEPTH, D_MODEL)),
        "norm_ffn": gain(ks[3], (DEPTH, D_MODEL)),
        "norm_ple": gain(ks[4], (DEPTH, D_MODEL)),
        "even_w_in": nrm(ks[5], (N_EVEN, D_MODEL, EVEN_IN), D_MODEL),
        "even_b_forget": jax.random.uniform(ks[6], (N_EVEN, FOX_HEADS), jnp.float32, minval=1.0, maxval=6.0),
        "even_conv_w": nrm(ks[7], (N_EVEN, CONV_W, CONV_CH), CONV_W),
        "even_q_gain": gain(ks[8], (N_EVEN, FOX_HEAD_DIM)),
        "even_k_gain": gain(ks[9], (N_EVEN, FOX_HEAD_DIM)),
        "even_w_out": nrm(ks[10], (N_EVEN, CONV_CH + FOX_DIM, D_MODEL), CONV_CH + FOX_DIM),
        "odd_w_in": nrm(ks[11], (N_ODD, D_MODEL, 2 * LRU_WIDTH), D_MODEL),
        "odd_conv_w": nrm(ks[12], (N_ODD, LRU_CONV_W, LRU_WIDTH), LRU_CONV_W),
        "odd_conv_b": small(ks[13], (N_ODD, LRU_WIDTH)),
        "odd_w_a": nrm(ks[14], (N_ODD, LRU_HEADS, LRU_BLOCK, LRU_BLOCK), LRU_BLOCK),
        "odd_b_a": small(ks[15], (N_ODD, LRU_WIDTH)),
        "odd_w_x": nrm(ks[16], (N_ODD, LRU_HEADS, LRU_BLOCK, LRU_BLOCK), LRU_BLOCK),
        "odd_b_x": small(ks[17], (N_ODD, LRU_WIDTH)),
        "odd_lru_param": lru_param,
        "odd_w_out": nrm(ks[18], (N_ODD, LRU_WIDTH, D_MODEL), LRU_WIDTH),
        "peer_w_query": nrm(ks[20], (DEPTH, D_MODEL, PEER_HEADS * QUERY_DIM), D_MODEL),
        "peer_sub_keys": nrm(ks[21], (DEPTH, 2, N_KEYS, HALF), HALF),
        "peer_u": nrm(ks[22], (DEPTH, N_EXPERTS, D_MODEL), D_MODEL),
        "peer_v": nrm(ks[23], (DEPTH, N_EXPERTS, D_MODEL), D_MODEL),
        "ple_w_up": nrm(ks[24], (DEPTH, PLE_DIM, D_MODEL), PLE_DIM),
        "ple_w_gate": nrm(ks[25], (DEPTH, D_MODEL, D_MODEL), D_MODEL),
    }


def reference(x, p, norm_mix, norm_ffn, norm_ple, even_w_in, even_b_forget, even_conv_w,
              even_q_gain, even_k_gain, even_w_out, odd_w_in, odd_conv_w, odd_conv_b,
              odd_w_a, odd_b_a, odd_w_x, odd_b_x, odd_lru_param, odd_w_out,
              peer_w_query, peer_sub_keys, peer_u, peer_v, ple_w_up, ple_w_gate):
    h = x
    for layer in range(DEPTH):
        j = layer // 2
        hn = rms_norm(h, norm_mix[layer])
        if layer % 2 == 0:
            h = h + mix_even(hn, even_w_in[j], even_b_forget[j], even_conv_w[j],
                             even_q_gain[j], even_k_gain[j], even_w_out[j])
        else:
            h = h + mix_odd(hn, odd_w_in[j], odd_conv_w[j], odd_conv_b[j], odd_w_a[j],
                            odd_b_a[j], odd_w_x[j], odd_b_x[j], odd_lru_param[j], odd_w_out[j])
        h = h + peer(rms_norm(h, norm_ffn[layer]), peer_w_query[layer], peer_sub_keys[layer],
                     peer_u[layer], peer_v[layer])
        gate = jax.nn.sigmoid(rms_norm(h, norm_ple[layer]) @ ple_w_gate[layer])
        h = h + (p[layer] @ ple_w_up[layer]) * gate
    return h
```

```python
import functools

import jax
import jax.numpy as jnp
from jax import lax
from jax.experimental import pallas as pl
from jax.experimental.pallas import tpu as pltpu

F32 = jnp.float32
BF16 = jnp.bfloat16
I32 = jnp.int32
EPS = 1e-6

LANES = 128
SUBLANES = 8
VMEM_LIMIT = 56 * 1024 * 1024

CONV_CH = 512
CONV_W = 3
FOX_HEADS = 8
FOX_HEAD_DIM = 64
FOX_DIM = FOX_HEADS * FOX_HEAD_DIM
LRU_HEADS = 8
LRU_BLOCK = 128
LRU_CONV_W = 4
LRU_C = 8.0
PEER_HEADS = 8
N_KEYS = 128
PEER_TOPK = 16
HALF = 128
ROUTE_TOKENS = 128
EXPERT_TOKENS = 8
DMA_UNROLL = 8


def _params(*sem):
    return pltpu.CompilerParams(dimension_semantics=sem, vmem_limit_bytes=VMEM_LIMIT)


def _rms(x, g):
    ms = jnp.mean(x * x, axis=-1, keepdims=True)
    return x * lax.rsqrt(ms + EPS) * g


def _gelu_tanh(x):
    return 0.5 * x * (1.0 + jnp.tanh(0.7978845608028654 * (x + 0.044715 * (x * x * x))))


def _split3(x):
    hi = x.astype(BF16)
    r1 = x - hi.astype(F32)
    mid = r1.astype(BF16)
    lo = (r1 - mid.astype(F32)).astype(BF16)
    return hi, mid, lo


def _norm_mm_kernel(*refs, has_t):
    if has_t:
        x_ref, g_ref, w_ref, wt_ref, o_ref, ot_ref, xn_ref = refs
    else:
        x_ref, g_ref, w_ref, o_ref, xn_ref = refs

    @pl.when(pl.program_id(1) == 0)
    def _():
        xn = _rms(x_ref[...], g_ref[...]).astype(BF16)
        xn_ref[...] = xn
        if has_t:
            ot_ref[...] = lax.dot_general(wt_ref[...], xn, (((1,), (1,)), ((), ())),
                                          preferred_element_type=F32)

    o_ref[...] = jnp.dot(xn_ref[...], w_ref[...], preferred_element_type=F32).astype(o_ref.dtype)


def norm_matmul(x, gain, w, wt=None, *, tm=512, tn=1024, out_dtype=F32):
    t, d = x.shape
    n = w.shape[1]
    tn = min(tn, n)
    in_specs = [pl.BlockSpec((tm, d), lambda i, j: (i, 0)),
                pl.BlockSpec((1, d), lambda i, j: (0, 0)),
                pl.BlockSpec((d, tn), lambda i, j: (0, j))]
    args = [x, gain.reshape(1, d), w]
    out_shape = [jax.ShapeDtypeStruct((t, n), out_dtype)]
    out_specs = [pl.BlockSpec((tm, tn), lambda i, j: (i, j))]
    if wt is not None:
        in_specs.append(pl.BlockSpec(wt.shape, lambda i, j: (0, 0)))
        args.append(wt)
        out_shape.append(jax.ShapeDtypeStruct((wt.shape[0], t), F32))
        out_specs.append(pl.BlockSpec((wt.shape[0], tm), lambda i, j: (0, i)))
    res = pl.pallas_call(
        functools.partial(_norm_mm_kernel, has_t=wt is not None),
        grid=(t // tm, n // tn),
        in_specs=in_specs, out_specs=out_specs, out_shape=out_shape,
        scratch_shapes=[pltpu.VMEM((tm, d), BF16)],
        compiler_params=_params("parallel", "arbitrary"),
        name="norm_matmul_t" if wt is not None else "norm_matmul",
    )(*args)
    return res if wt is not None else res[0]


def _conv_even_kernel(gb_ref, gc_ref, xa_ref, gch_ref, xah_ref, w_ref, o_ref, *, blocks_per_seq):
    i = pl.program_id(0)
    u = gc_ref[...] * xa_ref[...]
    uh = gch_ref[...] * xah_ref[...]
    uh = jnp.where(i % blocks_per_seq == 0, 0.0, uh)
    ext = jnp.concatenate([uh, u], axis=0)
    w = w_ref[...]
    y = (w[2:3] * ext[SUBLANES:]
         + w[1:2] * pltpu.roll(ext, 1, 0)[SUBLANES:]
         + w[0:1] * pltpu.roll(ext, 2, 0)[SUBLANES:])
    o_ref[...] = (gb_ref[...] * y).astype(o_ref.dtype)


def conv_even(z, conv_w, seq, *, ts=512):
    t = z.shape[0]
    c = CONV_CH
    hb = ts // SUBLANES

    def halo(col):
        return pl.BlockSpec((SUBLANES, c), lambda i: (jnp.maximum(i * hb - 1, 0), col))

    return pl.pallas_call(
        functools.partial(_conv_even_kernel, blocks_per_seq=seq // ts),
        grid=(t // ts,),
        in_specs=[pl.BlockSpec((ts, c), lambda i: (i, 0)),
                  pl.BlockSpec((ts, c), lambda i: (i, 1)),
                  pl.BlockSpec((ts, c), lambda i: (i, 2)),
                  halo(1), halo(2),
                  pl.BlockSpec((CONV_W, c), lambda i: (0, 0))],
        out_specs=pl.BlockSpec((ts, c), lambda i: (i, 0)),
        out_shape=jax.ShapeDtypeStruct((t, c), BF16),
        compiler_params=_params("parallel"),
        name="conv_even",
    )(z, z, z, z, z, conv_w)


def _head_norm(x, gain, bd):
    x2 = x * x
    hi = x2.astype(BF16)
    lo = (x2 - hi.astype(F32)).astype(BF16)
    ss = jnp.dot(hi, bd, preferred_element_type=F32) + jnp.dot(lo, bd, preferred_element_type=F32)
    return x * lax.rsqrt(ss * (1.0 / FOX_HEAD_DIM) + EPS) * gain


def _qkv_prep_kernel(q_ref, k_ref, v_ref, qg_ref, kg_ref, bd_ref, qo_ref, ko_ref, vo_ref):
    bd = bd_ref[...]
    scale = FOX_HEAD_DIM ** -0.5
    qo_ref[...] = (_head_norm(q_ref[...], qg_ref[...], bd) * scale).astype(BF16)
    ko_ref[...] = _head_norm(k_ref[...], kg_ref[...], bd).astype(BF16)
    vo_ref[...] = v_ref[...].astype(BF16)


def qkv_prep(z, q_gain, k_gain, *, ts=512):
    t = z.shape[0]
    c = FOX_DIM
    qg = jnp.tile(q_gain, FOX_HEADS).reshape(1, c)
    kg = jnp.tile(k_gain, FOX_HEADS).reshape(1, c)
    head = jnp.arange(c) // FOX_HEAD_DIM
    bd = (head[:, None] == head[None, :]).astype(BF16)
    blk = lambda col: pl.BlockSpec((ts, c), lambda i: (i, col))
    one = lambda shape: pl.BlockSpec(shape, lambda i: (0, 0))
    return pl.pallas_call(
        _qkv_prep_kernel,
        grid=(t // ts,),
        in_specs=[blk(3), blk(4), blk(5), one((1, c)), one((1, c)), one((c, c))],
        out_specs=[pl.BlockSpec((ts, c), lambda i: (i, 0))] * 3,
        out_shape=[jax.ShapeDtypeStruct((t, c), BF16)] * 3,
        compiler_params=_params("parallel"),
        name="qkv_prep",
    )(z, z, z, qg, kg, bd)


def _forget_cumsum_kernel(f_ref, b_ref, o_ref, ot_ref, *, seq):
    r = lax.broadcasted_iota(I32, (LANES, LANES), 0)
    c = lax.broadcasted_iota(I32, (LANES, LANES), 1)
    upper = (r <= c).astype(BF16)
    eye = (r == c).astype(BF16)
    b = b_ref[...]
    carry = jnp.zeros((FOX_HEADS, 1), F32)
    for ch in range(seq // LANES):
        x = f_ref[:, ch * LANES:(ch + 1) * LANES] + b
        ls = jnp.minimum(x, 0.0) - jnp.log1p(jnp.exp(-jnp.abs(x)))
        pre = sum(jnp.dot(p, upper, preferred_element_type=F32) for p in _split3(ls))
        out = pre + carry
        o_ref[:, ch * LANES:(ch + 1) * LANES] = out
        carry = out[:, LANES - 1:LANES]
        ot_ref[ch * LANES:(ch + 1) * LANES, :] = sum(
            lax.dot_general(eye, p, (((1,), (1,)), ((), ())), preferred_element_type=F32)
            for p in _split3(out))


def forget_cumsum(f_t, b_forget, batch, seq):
    return pl.pallas_call(
        functools.partial(_forget_cumsum_kernel, seq=seq),
        grid=(batch,),
        in_specs=[pl.BlockSpec((FOX_HEADS, seq), lambda b: (0, b)),
                  pl.BlockSpec((FOX_HEADS, 1), lambda b: (0, 0))],
        out_specs=[pl.BlockSpec((None, FOX_HEADS, seq), lambda b: (b, 0, 0)),
                   pl.BlockSpec((seq, FOX_HEADS), lambda b: (b, 0))],
        out_shape=[jax.ShapeDtypeStruct((batch, FOX_HEADS, seq), F32),
                   jax.ShapeDtypeStruct((batch * seq, FOX_HEADS), F32)],
        compiler_params=_params("parallel"),
        name="forget_cumsum",
    )(f_t, b_forget.reshape(FOX_HEADS, 1))


def _fox_attn_kernel(q_ref, k_ref, v_ref, fh_ref, fht_ref, o_ref, *, tq):
    pair = pl.program_id(1)
    i = pl.program_id(2)
    q = q_ref[...]
    lane = lax.broadcasted_iota(I32, (tq, LANES), 1)
    lane8 = lax.broadcasted_iota(I32, (tq, FOX_HEADS), 1)
    row = lax.broadcasted_iota(I32, (tq, tq), 0)
    col = lax.broadcasted_iota(I32, (tq, tq), 1)
    fht = fht_ref[...]
    heads = []
    for hh in range(2):
        first = lane < FOX_HEAD_DIM
        qm = jnp.where(first if hh == 0 else jnp.logical_not(first), q, jnp.zeros_like(q))
        fq = jnp.sum(jnp.where(lane8 == 2 * pair + hh, fht, 0.0), axis=-1, keepdims=True)

        def kv_step(j, carry, masked):
            m, l, acc = carry
            start = pl.multiple_of(j * tq, tq)
            kb = k_ref[pl.ds(start, tq), :]
            vb = v_ref[pl.ds(start, tq), :]
            s = lax.dot_general(qm, kb, (((1,), (1,)), ((), ())), preferred_element_type=F32)
            t = s - fh_ref[hh:hh + 1, pl.ds(start, tq)]
            if masked:
                t = jnp.where(col <= row, t, -jnp.inf)
            m_new = jnp.maximum(m, jnp.max(t, axis=-1, keepdims=True) + fq)
            p = jnp.exp(t + (fq - m_new))
            alpha = jnp.exp(m - m_new)
            l = alpha * l + jnp.sum(p, axis=-1, keepdims=True)
            acc = alpha * acc + jnp.dot(p.astype(BF16), vb, preferred_element_type=F32)
            return m_new, l, acc

        init = (jnp.full((tq, 1), -jnp.inf, F32), jnp.zeros((tq, 1), F32), jnp.zeros((tq, LANES), F32))
        carry = lax.fori_loop(0, i, functools.partial(kv_step, masked=False), init)
        _, l, acc = kv_step(i, carry, True)
        heads.append(acc / l)
    o_ref[...] = jnp.where(lane < FOX_HEAD_DIM, heads[0], heads[1]).astype(o_ref.dtype)


def fox_attention(q, k, v, fh, fht, batch, seq, *, tq=512):
    t = q.shape[0]
    nq = seq // tq
    pairs = FOX_HEADS // 2
    fh4 = fh.reshape(batch, pairs, 2, seq)
    return pl.pallas_call(
        functools.partial(_fox_attn_kernel, tq=tq),
        grid=(batch, pairs, nq),
        in_specs=[pl.BlockSpec((tq, LANES), lambda b, p, i: (b * nq + i, p)),
                  pl.BlockSpec((seq, LANES), lambda b, p, i: (b, p)),
                  pl.BlockSpec((seq, LANES), lambda b, p, i: (b, p)),
                  pl.BlockSpec((None, None, 2, seq), lambda b, p, i: (b, p, 0, 0)),
                  pl.BlockSpec((tq, FOX_HEADS), lambda b, p, i: (b * nq + i, 0))],
        out_specs=pl.BlockSpec((tq, LANES), lambda b, p, i: (b * nq + i, p)),
        out_shape=jax.ShapeDtypeStruct((t, FOX_DIM), BF16),
        compiler_params=_params("parallel", "parallel", "arbitrary"),
        name="fox_attention",
    )(q, k, v, fh4, fht)


def _out_proj2_kernel(h_ref, a_ref, b_ref, wa_ref, wb_ref, o_ref):
    o_ref[...] = (h_ref[...]
                  + jnp.dot(a_ref[...], wa_ref[...], preferred_element_type=F32)
                  + jnp.dot(b_ref[...], wb_ref[...], preferred_element_type=F32))


def out_proj2(h, a, b, wa, wb, *, tm=512):
    t, d = h.shape
    ka, kb = a.shape[1], b.shape[1]
    return pl.pallas_call(
        _out_proj2_kernel,
        grid=(t // tm,),
        in_specs=[pl.BlockSpec((tm, d), lambda i: (i, 0)),
                  pl.BlockSpec((tm, ka), lambda i: (i, 0)),
                  pl.BlockSpec((tm, kb), lambda i: (i, 0)),
                  pl.BlockSpec((ka, d), lambda i: (0, 0)),
                  pl.BlockSpec((kb, d), lambda i: (0, 0))],
        out_specs=pl.BlockSpec((tm, d), lambda i: (i, 0)),
        out_shape=jax.ShapeDtypeStruct((t, d), F32),
        compiler_params=_params("parallel"),
        name="out_proj2",
    )(h, a, b, wa, wb)


def _out_proj1_kernel(h_ref, a_ref, w_ref, o_ref):
    o_ref[...] = h_ref[...] + jnp.dot(a_ref[...], w_ref[...], preferred_element_type=F32)


def out_proj1(h, a, w, *, tm=512):
    t, d = h.shape
    ka = a.shape[1]
    return pl.pallas_call(
        _out_proj1_kernel,
        grid=(t // tm,),
        in_specs=[pl.BlockSpec((tm, d), lambda i: (i, 0)),
                  pl.BlockSpec((tm, ka), lambda i: (i, 0)),
                  pl.BlockSpec((ka, d), lambda i: (0, 0))],
        out_specs=pl.BlockSpec((tm, d), lambda i: (i, 0)),
        out_shape=jax.ShapeDtypeStruct((t, d), F32),
        compiler_params=_params("parallel"),
        name="out_proj1",
    )(h, a, w)


def _shift_rows(x, s, fill):
    rows = lax.broadcasted_iota(I32, x.shape, 0)
    return jnp.where(rows < s, fill, pltpu.roll(x, s, 0))


def _lru_kernel(gate_ref, xr_ref, cw_ref, cb_ref, wa_ref, ba_ref, wx_ref, bx_ref, lam_ref,
                o_ref, halo_ref, carry_ref, *, ts):
    @pl.when(pl.program_id(1) == 0)
    def _():
        halo_ref[...] = jnp.zeros_like(halo_ref)
        carry_ref[...] = jnp.zeros_like(carry_ref)

    x = xr_ref[...]
    ext = jnp.concatenate([halo_ref[...], x], axis=0)
    halo_ref[...] = x[ts - SUBLANES:, :]
    cw = cw_ref[...]
    xc = cb_ref[...] + cw[3:4] * x
    for k in range(LRU_CONV_W - 1):
        xc = xc + cw[k:k + 1] * pltpu.roll(ext, LRU_CONV_W - 1 - k, 0)[SUBLANES:]

    xcb = xc.astype(BF16)
    ra, ri = [], []
    for h in range(LRU_HEADS):
        xh = xcb[:, h * LRU_BLOCK:(h + 1) * LRU_BLOCK]
        ra.append(jnp.dot(xh, wa_ref[h], preferred_element_type=F32))
        ri.append(jnp.dot(xh, wx_ref[h], preferred_element_type=F32))
    r = jax.nn.sigmoid(jnp.concatenate(ra, axis=1) + ba_ref[...])
    gi = jax.nn.sigmoid(jnp.concatenate(ri, axis=1) + bx_ref[...])
    nl = -lam_ref[...]
    softplus = jnp.maximum(nl, 0.0) + jnp.log1p(jnp.exp(-jnp.abs(nl)))
    log_a = -LRU_C * r * softplus
    a = jnp.exp(log_a)
    mult = jnp.sqrt(-jnp.tanh(log_a) * (a * a + 1.0))
    u = mult * (gi * xc)

    s = 1
    while s < ts:
        u = a * _shift_rows(u, s, 0.0) + u
        a = a * _shift_rows(a, s, 1.0)
        s *= 2
    hs = u + a * carry_ref[0:1, :]
    carry_ref[...] = jnp.broadcast_to(hs[ts - 1:ts, :], carry_ref.shape)
    o_ref[...] = (hs * _gelu_tanh(gate_ref[...])).astype(o_ref.dtype)


def lru_mix(z, conv_w, conv_b, w_a, b_a, w_x, b_x, lam, batch, seq, *, ts=256):
    t = z.shape[0]
    w = z.shape[1] // 2
    nb = seq // ts
    row = lambda: pl.BlockSpec((1, w), lambda b, i: (0, 0))
    blockw = lambda: pl.BlockSpec((LRU_HEADS, LRU_BLOCK, LRU_BLOCK), lambda b, i: (0, 0, 0))
    return pl.pallas_call(
        functools.partial(_lru_kernel, ts=ts),
        grid=(batch, nb),
        in_specs=[pl.BlockSpec((ts, w), lambda b, i: (b * nb + i, 0)),
                  pl.BlockSpec((ts, w), lambda b, i: (b * nb + i, 1)),
                  pl.BlockSpec((LRU_CONV_W, w), lambda b, i: (0, 0)),
                  row(), blockw(), row(), blockw(), row(), row()],
        out_specs=pl.BlockSpec((ts, w), lambda b, i: (b * nb + i, 0)),
        out_shape=jax.ShapeDtypeStruct((t, w), BF16),
        scratch_shapes=[pltpu.VMEM((SUBLANES, w), F32), pltpu.VMEM((SUBLANES, w), F32)],
        compiler_params=_params("parallel", "arbitrary"),
        name="lru_mix",
    )(z, z, conv_w, conv_b.reshape(1, w), w_a.astype(BF16), b_a.reshape(1, w),
      w_x.astype(BF16), b_x.reshape(1, w), lam.reshape(1, w))


def _top_rows(vals, payload, k):
    nrow, nlane = vals.shape
    rio = lax.broadcasted_iota(I32, (nrow, nlane), 0).astype(F32)
    kio = lax.broadcasted_iota(I32, (k, nlane), 0)
    top_v = jnp.zeros((k, nlane), F32)
    top_p = jnp.zeros((k, nlane), F32)
    for r in range(k):
        m = jnp.max(vals, axis=0, keepdims=True)
        first = jnp.min(jnp.where(vals == m, rio, float(nrow)), axis=0, keepdims=True)
        sel = rio == first
        if payload is None:
            p = first
        else:
            p = jnp.sum(jnp.where(sel, payload, 0.0), axis=0, keepdims=True)
        vals = jnp.where(sel, -jnp.inf, vals)
        top_v = jnp.where(kio == r, m, top_v)
        top_p = jnp.where(kio == r, p, top_p)
    return top_v, top_p


def _peer_route_kernel(h_ref, g_ref, wq_ref, keys_ref, xn_ref, idx_ref, gate_ref):
    xn = _rms(h_ref[...], g_ref[...]).astype(BF16)
    xn_ref[...] = xn
    q = jnp.dot(xn, wq_ref[...], preferred_element_type=F32).astype(BF16)
    k = PEER_TOPK
    idx_rows, gate_rows = [], []
    for h in range(PEER_HEADS):
        tops = []
        for c in range(2):
            qhc = q[:, (2 * h + c) * HALF:(2 * h + c + 1) * HALF]
            sc = lax.dot_general(keys_ref[c], qhc, (((1,), (1,)), ((), ())), preferred_element_type=F32)
            tops.append(_top_rows(sc, None, k))
        (s1, i1), (s2, i2) = tops
        cand = jnp.concatenate([s1[a:a + 1, :] + s2 for a in range(k)], axis=0)
        expert = jnp.concatenate([i1[a:a + 1, :] * float(N_KEYS) + i2 for a in range(k)], axis=0)
        best_s, best_e = _top_rows(cand, expert, k)
        e = jnp.exp(best_s - jnp.max(best_s, axis=0, keepdims=True))
        gate_rows.append(e / jnp.sum(e, axis=0, keepdims=True))
        idx_rows.append(best_e)
    idx_t = jnp.concatenate(idx_rows, axis=0)
    gate_t = jnp.concatenate(gate_rows, axis=0)
    idx_ref[...] = idx_t.T.astype(I32)
    gate_ref[...] = gate_t.T


def peer_route(h, gain, wq, keys):
    t, d = h.shape
    tb = ROUTE_TOKENS
    nq = wq.shape[1]
    return pl.pallas_call(
        _peer_route_kernel,
        grid=(t // tb,),
        in_specs=[pl.BlockSpec((tb, d), lambda i: (i, 0)),
                  pl.BlockSpec((1, d), lambda i: (0, 0)),
                  pl.BlockSpec((d, nq), lambda i: (0, 0)),
                  pl.BlockSpec((2, N_KEYS, HALF), lambda i: (0, 0, 0))],
        out_specs=[pl.BlockSpec((tb, d), lambda i: (i, 0)),
                   pl.BlockSpec((tb, PEER_HEADS * PEER_TOPK), lambda i: (i, 0)),
                   pl.BlockSpec((tb, PEER_HEADS * PEER_TOPK), lambda i: (i, 0))],
        out_shape=[jax.ShapeDtypeStruct((t, d), BF16),
                   jax.ShapeDtypeStruct((t, PEER_HEADS * PEER_TOPK), I32),
                   jax.ShapeDtypeStruct((t, PEER_HEADS * PEER_TOPK), F32)],
        compiler_params=_params("parallel"),
        name="peer_route",
    )(h, gain.reshape(1, d), wq, keys)


def _peer_expert_kernel(idx_cur, idx_nxt, xn_ref, gate_ref, h_ref, tab_ref, o_ref, buf, sem,
                        *, tb, nblk, d):
    i = pl.program_id(0)
    slot = i % 2
    nsel = PEER_HEADS * PEER_TOPK
    nrow = tb * nsel

    def row_copy(e, s, r):
        return pltpu.make_async_copy(tab_ref.at[pl.ds(e, 1)], buf.at[s, pl.ds(r, 1)], sem.at[s])

    def issue(idx_ref, s):
        def body(g, c):
            for u in range(DMA_UNROLL):
                r = g * DMA_UNROLL + u
                row_copy(idx_ref[r], s, r).start()
            return c
        lax.fori_loop(0, nrow // DMA_UNROLL, body, 0)

    @pl.when(i == 0)
    def _():
        issue(idx_cur, 0)

    @pl.when(i + 1 < nblk)
    def _():
        issue(idx_nxt, 1 - slot)

    def wait_body(r, c):
        row_copy(0, slot, r).wait()
        return c
    lax.fori_loop(0, nrow, wait_body, 0)

    for t in range(tb):
        rows = pl.ds(t * nsel, nsel)
        u_t = buf[slot, rows, pl.ds(0, d)].astype(BF16)
        x8 = jnp.broadcast_to(xn_ref[t:t + 1, :], (SUBLANES, d))
        act = lax.dot_general(x8, u_t, (((1,), (1,)), ((), ())), preferred_element_type=F32)
        w = (gate_ref[t:t + 1, :] * _gelu_tanh(act)).astype(BF16)
        v_t = buf[slot, rows, pl.ds(d, d)].astype(BF16)
        out = jnp.dot(w, v_t, preferred_element_type=F32)
        o_ref[t:t + 1, :] = h_ref[t:t + 1, :] + out[0:1, :]


def peer_expert(h, xn, idx, gate, table):
    t, d = h.shape
    tb = EXPERT_TOKENS
    nsel = PEER_HEADS * PEER_TOPK
    nblk = t // tb
    idx_flat = idx.reshape(t * nsel)
    smem = lambda f: pl.BlockSpec((tb * nsel,), f, memory_space=pltpu.SMEM)
    return pl.pallas_call(
        functools.partial(_peer_expert_kernel, tb=tb, nblk=nblk, d=d),
        grid=(nblk,),
        in_specs=[smem(lambda i: (i,)),
                  smem(lambda i: (jnp.minimum(i + 1, nblk - 1),)),
                  pl.BlockSpec((tb, d), lambda i: (i, 0)),
                  pl.BlockSpec((tb, nsel), lambda i: (i, 0)),
                  pl.BlockSpec((tb, d), lambda i: (i, 0)),
                  pl.BlockSpec(memory_space=pl.ANY)],
        out_specs=pl.BlockSpec((tb, d), lambda i: (i, 0)),
        out_shape=jax.ShapeDtypeStruct((t, d), F32),
        scratch_shapes=[pltpu.VMEM((2, tb * nsel, 2 * d), F32),
                        pltpu.SemaphoreType.DMA((2,))],
        compiler_params=_params("arbitrary"),
        name="peer_expert",
    )(idx_flat, idx_flat, xn, gate, h, table)


def _ple_kernel(h_ref, g_ref, p_ref, wg_ref, wu_ref, o_ref):
    h = h_ref[...]
    xn = _rms(h, g_ref[...]).astype(BF16)
    gate = jax.nn.sigmoid(jnp.dot(xn, wg_ref[...], preferred_element_type=F32))
    up = jnp.dot(p_ref[...].astype(BF16), wu_ref[...], preferred_element_type=F32)
    o_ref[...] = h + up * gate


def ple(h, gain, p, w_gate, w_up, *, tm=512):
    t, d = h.shape
    pd = p.shape[1]
    return pl.pallas_call(
        _ple_kernel,
        grid=(t // tm,),
        in_specs=[pl.BlockSpec((tm, d), lambda i: (i, 0)),
                  pl.BlockSpec((1, d), lambda i: (0, 0)),
                  pl.BlockSpec((tm, pd), lambda i: (i, 0)),
                  pl.BlockSpec((d, d), lambda i: (0, 0)),
                  pl.BlockSpec((pd, d), lambda i: (0, 0))],
        out_specs=pl.BlockSpec((tm, d), lambda i: (i, 0)),
        out_shape=jax.ShapeDtypeStruct((t, d), F32),
        compiler_params=_params("parallel"),
        name="ple",
    )(h, gain.reshape(1, d), p, w_gate, w_up)


def _mix_even(h, gain, w_in, b_forget, conv_w, q_gain, k_gain, w_out, batch, seq):
    main = 3 * CONV_CH + 3 * FOX_DIM
    w_main = w_in[:, :main].astype(BF16)
    w_f = w_in[:, main:].T.astype(BF16)
    z, f_t = norm_matmul(h, gain, w_main, w_f)
    ya = conv_even(z, conv_w, seq)
    q, k, v = qkv_prep(z, q_gain, k_gain)
    fh, fht = forget_cumsum(f_t, b_forget, batch, seq)
    yb = fox_attention(q, k, v, fh, fht, batch, seq)
    wo = w_out.astype(BF16)
    return out_proj2(h, ya, yb, wo[:CONV_CH], wo[CONV_CH:])


def _mix_odd(h, gain, w_in, conv_w, conv_b, w_a, b_a, w_x, b_x, lam, w_out, batch, seq):
    z = norm_matmul(h, gain, w_in.astype(BF16))
    y = lru_mix(z, conv_w, conv_b, w_a, b_a, w_x, b_x, lam, batch, seq)
    return out_proj1(h, y, w_out.astype(BF16))


def _peer(h, gain, w_query, sub_keys, u, v):
    xn, idx, gate = peer_route(h, gain, w_query.astype(BF16), sub_keys.astype(BF16))
    table = jnp.concatenate([u, v], axis=1)
    return peer_expert(h, xn, idx, gate, table)


def kernel(x, p, norm_mix, norm_ffn, norm_ple, even_w_in, even_b_forget, even_conv_w, even_q_gain,
           even_k_gain, even_w_out, odd_w_in, odd_conv_w, odd_conv_b, odd_w_a, odd_b_a, odd_w_x,
           odd_b_x, odd_lru_param, odd_w_out, peer_w_query, peer_sub_keys, peer_u, peer_v,
           ple_w_up, ple_w_gate):
    batch, seq, d = x.shape
    depth = p.shape[0]
    h = x.reshape(batch * seq, d)
    for layer in range(depth):
        j = layer // 2
        if layer % 2 == 0:
            h = _mix_even(h, norm_mix[layer], even_w_in[j], even_b_forget[j], even_conv_w[j],
                          even_q_gain[j], even_k_gain[j], even_w_out[j], batch, seq)
        else:
            h = _mix_odd(h, norm_mix[layer], odd_w_in[j], odd_conv_w[j], odd_conv_b[j], odd_w_a[j],
                         odd_b_a[j], odd_w_x[j], odd_b_x[j], odd_lru_param[j], odd_w_out[j], batch, seq)
        h = _peer(h, norm_ffn[layer], peer_w_query[layer], peer_sub_keys[layer], peer_u[layer],
                  peer_v[layer])
        h = ple(h, norm_ple[layer], p[layer].reshape(batch * seq, -1), ple_w_gate[layer].astype(BF16),
                ple_w_up[layer].astype(BF16))
    return h.reshape(batch, seq, d)
```

```python
import functools

import jax
import jax.numpy as jnp
from jax import lax
from jax.experimental import pallas as pl
from jax.experimental.pallas import tpu as pltpu

F32 = jnp.float32
BF16 = jnp.bfloat16
I32 = jnp.int32
EPS = 1e-6

LANES = 128
SUBLANES = 8
VMEM_LIMIT = 56 * 1024 * 1024

CONV_CH = 512
CONV_W = 3
FOX_HEADS = 8
FOX_HEAD_DIM = 64
FOX_DIM = FOX_HEADS * FOX_HEAD_DIM
LRU_HEADS = 8
LRU_BLOCK = 128
LRU_CONV_W = 4
LRU_C = 8.0
PEER_HEADS = 8
N_KEYS = 128
PEER_TOPK = 16
HALF = 128
ROUTE_TOKENS = 128
EXPERT_TOKENS = 16
GATHER_SLOTS = 3


def _params(*sem):
    return pltpu.CompilerParams(dimension_semantics=sem, vmem_limit_bytes=VMEM_LIMIT)


def _rms(x, g):
    ms = jnp.mean(x * x, axis=-1, keepdims=True)
    return x * lax.rsqrt(ms + EPS) * g


def _gelu_tanh(x):
    return 0.5 * x * (1.0 + jnp.tanh(0.7978845608028654 * (x + 0.044715 * (x * x * x))))


def _split3(x):
    hi = x.astype(BF16)
    r1 = x - hi.astype(F32)
    mid = r1.astype(BF16)
    lo = (r1 - mid.astype(F32)).astype(BF16)
    return hi, mid, lo


def _norm_mm_kernel(*refs, has_t):
    if has_t:
        x_ref, g_ref, w_ref, wt_ref, o_ref, ot_ref, xn_ref = refs
    else:
        x_ref, g_ref, w_ref, o_ref, xn_ref = refs

    @pl.when(pl.program_id(1) == 0)
    def _():
        xn = _rms(x_ref[...], g_ref[...]).astype(BF16)
        xn_ref[...] = xn
        if has_t:
            ot_ref[...] = lax.dot_general(wt_ref[...], xn, (((1,), (1,)), ((), ())),
                                          preferred_element_type=F32)

    o_ref[...] = jnp.dot(xn_ref[...], w_ref[...], preferred_element_type=F32).astype(o_ref.dtype)


def norm_matmul(x, gain, w, wt=None, *, tm=512, tn=1024, out_dtype=F32):
    t, d = x.shape
    n = w.shape[1]
    tn = min(tn, n)
    in_specs = [pl.BlockSpec((tm, d), lambda i, j: (i, 0)),
                pl.BlockSpec((1, d), lambda i, j: (0, 0)),
                pl.BlockSpec((d, tn), lambda i, j: (0, j))]
    args = [x, gain.reshape(1, d), w]
    out_shape = [jax.ShapeDtypeStruct((t, n), out_dtype)]
    out_specs = [pl.BlockSpec((tm, tn), lambda i, j: (i, j))]
    if wt is not None:
        in_specs.append(pl.BlockSpec(wt.shape, lambda i, j: (0, 0)))
        args.append(wt)
        out_shape.append(jax.ShapeDtypeStruct((wt.shape[0], t), F32))
        out_specs.append(pl.BlockSpec((wt.shape[0], tm), lambda i, j: (0, i)))
    res = pl.pallas_call(
        functools.partial(_norm_mm_kernel, has_t=wt is not None),
        grid=(t // tm, n // tn),
        in_specs=in_specs, out_specs=out_specs, out_shape=out_shape,
        scratch_shapes=[pltpu.VMEM((tm, d), BF16)],
        compiler_params=_params("parallel", "arbitrary"),
        name="norm_matmul_t" if wt is not None else "norm_matmul",
    )(*args)
    return res if wt is not None else res[0]


def _conv_even_kernel(gb_ref, gc_ref, xa_ref, gch_ref, xah_ref, w_ref, o_ref, *, blocks_per_seq):
    i = pl.program_id(0)
    u = gc_ref[...] * xa_ref[...]
    uh = gch_ref[...] * xah_ref[...]
    uh = jnp.where(i % blocks_per_seq == 0, 0.0, uh)
    ext = jnp.concatenate([uh, u], axis=0)
    w = w_ref[...]
    y = (w[2:3] * ext[SUBLANES:]
         + w[1:2] * pltpu.roll(ext, 1, 0)[SUBLANES:]
         + w[0:1] * pltpu.roll(ext, 2, 0)[SUBLANES:])
    o_ref[...] = (gb_ref[...] * y).astype(o_ref.dtype)


def conv_even(z, conv_w, seq, *, ts=512):
    t = z.shape[0]
    c = CONV_CH
    hb = ts // SUBLANES

    def halo(col):
        return pl.BlockSpec((SUBLANES, c), lambda i: (jnp.maximum(i * hb - 1, 0), col))

    return pl.pallas_call(
        functools.partial(_conv_even_kernel, blocks_per_seq=seq // ts),
        grid=(t // ts,),
        in_specs=[pl.BlockSpec((ts, c), lambda i: (i, 0)),
                  pl.BlockSpec((ts, c), lambda i: (i, 1)),
                  pl.BlockSpec((ts, c), lambda i: (i, 2)),
                  halo(1), halo(2),
                  pl.BlockSpec((CONV_W, c), lambda i: (0, 0))],
        out_specs=pl.BlockSpec((ts, c), lambda i: (i, 0)),
        out_shape=jax.ShapeDtypeStruct((t, c), BF16),
        compiler_params=_params("parallel"),
        name="conv_even",
    )(z, z, z, z, z, conv_w)


def _head_norm(x, gain, bd):
    x2 = x * x
    hi = x2.astype(BF16)
    lo = (x2 - hi.astype(F32)).astype(BF16)
    ss = jnp.dot(hi, bd, preferred_element_type=F32) + jnp.dot(lo, bd, preferred_element_type=F32)
    return x * lax.rsqrt(ss * (1.0 / FOX_HEAD_DIM) + EPS) * gain


def _qkv_prep_kernel(q_ref, k_ref, v_ref, qg_ref, kg_ref, bd_ref, qo_ref, ko_ref, vo_ref):
    bd = bd_ref[...]
    scale = FOX_HEAD_DIM ** -0.5
    qo_ref[...] = (_head_norm(q_ref[...], qg_ref[...], bd) * scale).astype(BF16)
    ko_ref[...] = _head_norm(k_ref[...], kg_ref[...], bd).astype(BF16)
    vo_ref[...] = v_ref[...].astype(BF16)


def qkv_prep(z, q_gain, k_gain, *, ts=512):
    t = z.shape[0]
    c = FOX_DIM
    qg = jnp.tile(q_gain, FOX_HEADS).reshape(1, c)
    kg = jnp.tile(k_gain, FOX_HEADS).reshape(1, c)
    head = jnp.arange(c) // FOX_HEAD_DIM
    bd = (head[:, None] == head[None, :]).astype(BF16)
    blk = lambda col: pl.BlockSpec((ts, c), lambda i: (i, col))
    one = lambda shape: pl.BlockSpec(shape, lambda i: (0, 0))
    return pl.pallas_call(
        _qkv_prep_kernel,
        grid=(t // ts,),
        in_specs=[blk(3), blk(4), blk(5), one((1, c)), one((1, c)), one((c, c))],
        out_specs=[pl.BlockSpec((ts, c), lambda i: (i, 0))] * 3,
        out_shape=[jax.ShapeDtypeStruct((t, c), BF16)] * 3,
        compiler_params=_params("parallel"),
        name="qkv_prep",
    )(z, z, z, qg, kg, bd)


def _forget_cumsum_kernel(f_ref, b_ref, o_ref, ot_ref, *, seq):
    r = lax.broadcasted_iota(I32, (LANES, LANES), 0)
    c = lax.broadcasted_iota(I32, (LANES, LANES), 1)
    upper = (r <= c).astype(BF16)
    eye = (r == c).astype(BF16)
    b = b_ref[...]
    carry = jnp.zeros((FOX_HEADS, 1), F32)
    for ch in range(seq // LANES):
        x = f_ref[:, ch * LANES:(ch + 1) * LANES] + b
        ls = jnp.minimum(x, 0.0) - jnp.log1p(jnp.exp(-jnp.abs(x)))
        pre = sum(jnp.dot(p, upper, preferred_element_type=F32) for p in _split3(ls))
        out = pre + carry
        o_ref[:, ch * LANES:(ch + 1) * LANES] = out
        carry = out[:, LANES - 1:LANES]
        ot_ref[ch * LANES:(ch + 1) * LANES, :] = sum(
            lax.dot_general(eye, p, (((1,), (1,)), ((), ())), preferred_element_type=F32)
            for p in _split3(out))


def forget_cumsum(f_t, b_forget, batch, seq):
    return pl.pallas_call(
        functools.partial(_forget_cumsum_kernel, seq=seq),
        grid=(batch,),
        in_specs=[pl.BlockSpec((FOX_HEADS, seq), lambda b: (0, b)),
                  pl.BlockSpec((FOX_HEADS, 1), lambda b: (0, 0))],
        out_specs=[pl.BlockSpec((None, FOX_HEADS, seq), lambda b: (b, 0, 0)),
                   pl.BlockSpec((seq, FOX_HEADS), lambda b: (b, 0))],
        out_shape=[jax.ShapeDtypeStruct((batch, FOX_HEADS, seq), F32),
                   jax.ShapeDtypeStruct((batch * seq, FOX_HEADS), F32)],
        compiler_params=_params("parallel"),
        name="forget_cumsum",
    )(f_t, b_forget.reshape(FOX_HEADS, 1))


def _fox_attn_kernel(q_ref, k_ref, v_ref, fh_ref, fht_ref, o_ref, *, tq):
    pair = pl.program_id(1)
    i = pl.program_id(2)
    q = q_ref[...]
    lane = lax.broadcasted_iota(I32, (tq, LANES), 1)
    lane8 = lax.broadcasted_iota(I32, (tq, FOX_HEADS), 1)
    row = lax.broadcasted_iota(I32, (tq, tq), 0)
    col = lax.broadcasted_iota(I32, (tq, tq), 1)
    fht = fht_ref[...]
    heads = []
    for hh in range(2):
        first = lane < FOX_HEAD_DIM
        qm = jnp.where(first if hh == 0 else jnp.logical_not(first), q, jnp.zeros_like(q))
        fq = jnp.sum(jnp.where(lane8 == 2 * pair + hh, fht, 0.0), axis=-1, keepdims=True)

        def kv_step(j, carry, masked):
            m, l, acc = carry
            start = pl.multiple_of(j * tq, tq)
            kb = k_ref[pl.ds(start, tq), :]
            vb = v_ref[pl.ds(start, tq), :]
            s = lax.dot_general(qm, kb, (((1,), (1,)), ((), ())), preferred_element_type=F32)
            t = s - fh_ref[hh:hh + 1, pl.ds(start, tq)]
            if masked:
                t = jnp.where(col <= row, t, -jnp.inf)
            m_new = jnp.maximum(m, jnp.max(t, axis=-1, keepdims=True) + fq)
            p = jnp.exp(t + (fq - m_new))
            alpha = jnp.exp(m - m_new)
            l = alpha * l + jnp.sum(p, axis=-1, keepdims=True)
            acc = alpha * acc + jnp.dot(p.astype(BF16), vb, preferred_element_type=F32)
            return m_new, l, acc

        init = (jnp.full((tq, 1), -jnp.inf, F32), jnp.zeros((tq, 1), F32), jnp.zeros((tq, LANES), F32))
        carry = lax.fori_loop(0, i, functools.partial(kv_step, masked=False), init)
        _, l, acc = kv_step(i, carry, True)
        heads.append(acc / l)
    o_ref[...] = jnp.where(lane < FOX_HEAD_DIM, heads[0], heads[1]).astype(o_ref.dtype)


def fox_attention(q, k, v, fh, fht, batch, seq, *, tq=512):
    t = q.shape[0]
    nq = seq // tq
    pairs = FOX_HEADS // 2
    fh4 = fh.reshape(batch, pairs, 2, seq)
    return pl.pallas_call(
        functools.partial(_fox_attn_kernel, tq=tq),
        grid=(batch, pairs, nq),
        in_specs=[pl.BlockSpec((tq, LANES), lambda b, p, i: (b * nq + i, p)),
                  pl.BlockSpec((seq, LANES), lambda b, p, i: (b, p)),
                  pl.BlockSpec((seq, LANES), lambda b, p, i: (b, p)),
                  pl.BlockSpec((None, None, 2, seq), lambda b, p, i: (b, p, 0, 0)),
                  pl.BlockSpec((tq, FOX_HEADS), lambda b, p, i: (b * nq + i, 0))],
        out_specs=pl.BlockSpec((tq, LANES), lambda b, p, i: (b * nq + i, p)),
        out_shape=jax.ShapeDtypeStruct((t, FOX_DIM), BF16),
        compiler_params=_params("parallel", "parallel", "arbitrary"),
        name="fox_attention",
    )(q, k, v, fh4, fht)


def _out_proj2_kernel(h_ref, a_ref, b_ref, wa_ref, wb_ref, o_ref):
    o_ref[...] = (h_ref[...]
                  + jnp.dot(a_ref[...], wa_ref[...], preferred_element_type=F32)
                  + jnp.dot(b_ref[...], wb_ref[...], preferred_element_type=F32))


def out_proj2(h, a, b, wa, wb, *, tm=512):
    t, d = h.shape
    ka, kb = a.shape[1], b.shape[1]
    return pl.pallas_call(
        _out_proj2_kernel,
        grid=(t // tm,),
        in_specs=[pl.BlockSpec((tm, d), lambda i: (i, 0)),
                  pl.BlockSpec((tm, ka), lambda i: (i, 0)),
                  pl.BlockSpec((tm, kb), lambda i: (i, 0)),
                  pl.BlockSpec((ka, d), lambda i: (0, 0)),
                  pl.BlockSpec((kb, d), lambda i: (0, 0))],
        out_specs=pl.BlockSpec((tm, d), lambda i: (i, 0)),
        out_shape=jax.ShapeDtypeStruct((t, d), F32),
        compiler_params=_params("parallel"),
        name="out_proj2",
    )(h, a, b, wa, wb)


def _out_proj1_kernel(h_ref, a_ref, w_ref, o_ref):
    o_ref[...] = h_ref[...] + jnp.dot(a_ref[...], w_ref[...], preferred_element_type=F32)


def out_proj1(h, a, w, *, tm=512):
    t, d = h.shape
    ka = a.shape[1]
    return pl.pallas_call(
        _out_proj1_kernel,
        grid=(t // tm,),
        in_specs=[pl.BlockSpec((tm, d), lambda i: (i, 0)),
                  pl.BlockSpec((tm, ka), lambda i: (i, 0)),
                  pl.BlockSpec((ka, d), lambda i: (0, 0))],
        out_specs=pl.BlockSpec((tm, d), lambda i: (i, 0)),
        out_shape=jax.ShapeDtypeStruct((t, d), F32),
        compiler_params=_params("parallel"),
        name="out_proj1",
    )(h, a, w)


def _shift_rows(x, s, fill):
    rows = lax.broadcasted_iota(I32, x.shape, 0)
    return jnp.where(rows < s, fill, pltpu.roll(x, s, 0))


def _lru_kernel(gate_ref, xr_ref, cw_ref, cb_ref, wa_ref, ba_ref, wx_ref, bx_ref, lam_ref,
                o_ref, halo_ref, carry_ref, *, ts):
    @pl.when(pl.program_id(1) == 0)
    def _():
        halo_ref[...] = jnp.zeros_like(halo_ref)
        carry_ref[...] = jnp.zeros_like(carry_ref)

    x = xr_ref[...]
    ext = jnp.concatenate([halo_ref[...], x], axis=0)
    halo_ref[...] = x[ts - SUBLANES:, :]
    cw = cw_ref[...]
    xc = cb_ref[...] + cw[3:4] * x
    for k in range(LRU_CONV_W - 1):
        xc = xc + cw[k:k + 1] * pltpu.roll(ext, LRU_CONV_W - 1 - k, 0)[SUBLANES:]

    xcb = xc.astype(BF16)
    ra, ri = [], []
    for h in range(LRU_HEADS):
        xh = xcb[:, h * LRU_BLOCK:(h + 1) * LRU_BLOCK]
        ra.append(jnp.dot(xh, wa_ref[h], preferred_element_type=F32))
        ri.append(jnp.dot(xh, wx_ref[h], preferred_element_type=F32))
    r = jax.nn.sigmoid(jnp.concatenate(ra, axis=1) + ba_ref[...])
    gi = jax.nn.sigmoid(jnp.concatenate(ri, axis=1) + bx_ref[...])
    nl = -lam_ref[...]
    softplus = jnp.maximum(nl, 0.0) + jnp.log1p(jnp.exp(-jnp.abs(nl)))
    log_a = -LRU_C * r * softplus
    a = jnp.exp(log_a)
    mult = jnp.sqrt(-jnp.tanh(log_a) * (a * a + 1.0))
    u = mult * (gi * xc)

    s = 1
    while s < ts:
        u = a * _shift_rows(u, s, 0.0) + u
        a = a * _shift_rows(a, s, 1.0)
        s *= 2
    hs = u + a * carry_ref[0:1, :]
    carry_ref[...] = jnp.broadcast_to(hs[ts - 1:ts, :], carry_ref.shape)
    o_ref[...] = (hs * _gelu_tanh(gate_ref[...])).astype(o_ref.dtype)


def lru_mix(z, conv_w, conv_b, w_a, b_a, w_x, b_x, lam, batch, seq, *, ts=256):
    t = z.shape[0]
    w = z.shape[1] // 2
    nb = seq // ts
    row = lambda: pl.BlockSpec((1, w), lambda b, i: (0, 0))
    blockw = lambda: pl.BlockSpec((LRU_HEADS, LRU_BLOCK, LRU_BLOCK), lambda b, i: (0, 0, 0))
    return pl.pallas_call(
        functools.partial(_lru_kernel, ts=ts),
        grid=(batch, nb),
        in_specs=[pl.BlockSpec((ts, w), lambda b, i: (b * nb + i, 0)),
                  pl.BlockSpec((ts, w), lambda b, i: (b * nb + i, 1)),
                  pl.BlockSpec((LRU_CONV_W, w), lambda b, i: (0, 0)),
                  row(), blockw(), row(), blockw(), row(), row()],
        out_specs=pl.BlockSpec((ts, w), lambda b, i: (b * nb + i, 0)),
        out_shape=jax.ShapeDtypeStruct((t, w), BF16),
        scratch_shapes=[pltpu.VMEM((SUBLANES, w), F32), pltpu.VMEM((SUBLANES, w), F32)],
        compiler_params=_params("parallel", "arbitrary"),
        name="lru_mix",
    )(z, z, conv_w, conv_b.reshape(1, w), w_a.astype(BF16), b_a.reshape(1, w),
      w_x.astype(BF16), b_x.reshape(1, w), lam.reshape(1, w))


def _top_rows(vals, payload, k):
    nrow, nlane = vals.shape
    rio = lax.broadcasted_iota(I32, (nrow, nlane), 0).astype(F32)
    kio = lax.broadcasted_iota(I32, (k, nlane), 0)
    top_v = jnp.zeros((k, nlane), F32)
    top_p = jnp.zeros((k, nlane), F32)
    for r in range(k):
        m = jnp.max(vals, axis=0, keepdims=True)
        first = jnp.min(jnp.where(vals == m, rio, float(nrow)), axis=0, keepdims=True)
        sel = rio == first
        if payload is None:
            p = first
        else:
            p = jnp.sum(jnp.where(sel, payload, 0.0), axis=0, keepdims=True)
        vals = jnp.where(sel, -jnp.inf, vals)
        top_v = jnp.where(kio == r, m, top_v)
        top_p = jnp.where(kio == r, p, top_p)
    return top_v, top_p


def _peer_route_kernel(h_ref, g_ref, wq_ref, keys_ref, xn_ref, idx_ref, gate_ref):
    xn = _rms(h_ref[...], g_ref[...]).astype(BF16)
    xn_ref[...] = xn
    q = jnp.dot(xn, wq_ref[...], preferred_element_type=F32).astype(BF16)
    k = PEER_TOPK
    idx_rows, gate_rows = [], []
    for h in range(PEER_HEADS):
        tops = []
        for c in range(2):
            qhc = q[:, (2 * h + c) * HALF:(2 * h + c + 1) * HALF]
            sc = lax.dot_general(keys_ref[c], qhc, (((1,), (1,)), ((), ())), preferred_element_type=F32)
            tops.append(_top_rows(sc, None, k))
        (s1, i1), (s2, i2) = tops
        cand = jnp.concatenate([s1[a:a + 1, :] + s2 for a in range(k)], axis=0)
        expert = jnp.concatenate([i1[a:a + 1, :] * float(N_KEYS) + i2 for a in range(k)], axis=0)
        best_s, best_e = _top_rows(cand, expert, k)
        e = jnp.exp(best_s - jnp.max(best_s, axis=0, keepdims=True))
        gate_rows.append(e / jnp.sum(e, axis=0, keepdims=True))
        idx_rows.append(best_e)
    idx_t = jnp.concatenate(idx_rows, axis=0)
    gate_t = jnp.concatenate(gate_rows, axis=0)
    idx_ref[...] = idx_t.T.astype(I32)
    gate_ref[...] = gate_t.T


def peer_route(h, gain, wq, keys):
    t, d = h.shape
    tb = ROUTE_TOKENS
    nq = wq.shape[1]
    return pl.pallas_call(
        _peer_route_kernel,
        grid=(t // tb,),
        in_specs=[pl.BlockSpec((tb, d), lambda i: (i, 0)),
                  pl.BlockSpec((1, d), lambda i: (0, 0)),
                  pl.BlockSpec((d, nq), lambda i: (0, 0)),
                  pl.BlockSpec((2, N_KEYS, HALF), lambda i: (0, 0, 0))],
        out_specs=[pl.BlockSpec((tb, d), lambda i: (i, 0)),
                   pl.BlockSpec((tb, PEER_HEADS * PEER_TOPK), lambda i: (i, 0)),
                   pl.BlockSpec((tb, PEER_HEADS * PEER_TOPK), lambda i: (i, 0))],
        out_shape=[jax.ShapeDtypeStruct((t, d), BF16),
                   jax.ShapeDtypeStruct((t, PEER_HEADS * PEER_TOPK), I32),
                   jax.ShapeDtypeStruct((t, PEER_HEADS * PEER_TOPK), F32)],
        compiler_params=_params("parallel"),
        name="peer_route",
    )(h, gain.reshape(1, d), wq, keys)


def _peer_expert_kernel(idx0, idx1, idx2, xn_ref, gate_ref, h_ref, tab_ref, o_ref, buf, vstash, sem,
                        *, tb, nblk, d):
    i = pl.program_id(0)
    slot = i % GATHER_SLOTS
    ahead = (i + 2) % GATHER_SLOTS
    nsel = PEER_HEADS * PEER_TOPK
    tile = SUBLANES
    nchunk = d // LANES
    per_tok = nsel * tile
    per_chunk = nsel // nchunk

    def tile_copy(e, s, r):
        return pltpu.make_async_copy(
            tab_ref.at[pl.ds(pl.multiple_of(e * tile, tile), tile)],
            buf.at[s, pl.ds(pl.multiple_of(r * tile, tile), tile)], sem.at[s])

    def slot_wait(s):
        pltpu.make_async_copy(tab_ref.at[pl.ds(0, tb * per_tok)], buf.at[s], sem.at[s]).wait()

    @pl.when(i == 0)
    def _():
        def prime(r, c):
            tile_copy(idx0[r], 0, r).start()
            tile_copy(idx1[r], 1, r).start()
            return c
        lax.fori_loop(0, tb * nsel, prime, 0)

    slot_wait(slot)

    def chunk_words(t, c):
        return buf[slot, pl.ds(t * per_tok + c, nsel, stride=tile), :]

    acts, ws = {}, {}
    for t in range(tb + 2):
        if t < tb:
            x8 = jnp.broadcast_to(xn_ref[t:t + 1, :], (SUBLANES, d))
            act = jnp.zeros((SUBLANES, nsel), F32)
        for c in range(nchunk):
            if t < tb:
                for n in range(c * per_chunk, (c + 1) * per_chunk):
                    tile_copy(idx2[t * nsel + n], ahead, t * nsel + n).start(priority=n % 2)
                words = chunk_words(t, c)
                u_c = lax.bitcast_convert_type(words & jnp.int32(-65536), F32).astype(BF16)
                vstash[t % 3, c] = lax.bitcast_convert_type(words << 16, F32).astype(BF16)
                act = act + lax.dot_general(x8[:, c * LANES:(c + 1) * LANES], u_c, (((1,), (1,)), ((), ())),
                                            preferred_element_type=F32)
            if t >= 2:
                out = jnp.dot(ws[t - 2], vstash[(t - 2) % 3, c], preferred_element_type=F32)
                cols = slice(c * LANES, (c + 1) * LANES)
                o_ref[t - 2:t - 1, cols] = h_ref[t - 2:t - 1, cols] + out[0:1, :]
        if t < tb:
            acts[t] = act
        if 1 <= t <= tb:
            ws[t - 1] = (gate_ref[t - 1:t, :] * _gelu_tanh(acts.pop(t - 1))).astype(BF16)

    @pl.when(i == nblk - 1)
    def _():
        slot_wait(nblk % GATHER_SLOTS)
        slot_wait((nblk + 1) % GATHER_SLOTS)


def pack_expert_table(u, v):
    ne, d = u.shape
    ub = lax.bitcast_convert_type(u.astype(BF16), jnp.uint16).astype(jnp.uint32)
    vb = lax.bitcast_convert_type(v.astype(BF16), jnp.uint16).astype(jnp.uint32)
    words = lax.bitcast_convert_type((ub << 16) | vb, I32)
    return words.reshape(ne * (d // LANES), LANES)


def peer_expert(h, xn, idx, gate, table):
    t, d = h.shape
    assert d == SUBLANES * LANES
    tb = EXPERT_TOKENS
    nsel = PEER_HEADS * PEER_TOPK
    nblk = t // tb
    assert nblk >= 2
    idx_flat = idx.reshape(t * nsel)
    smem = lambda k: pl.BlockSpec((tb * nsel,), lambda i: (jnp.minimum(i + k, nblk - 1),),
                                  memory_space=pltpu.SMEM)
    return pl.pallas_call(
        functools.partial(_peer_expert_kernel, tb=tb, nblk=nblk, d=d),
        grid=(nblk,),
        in_specs=[smem(0), smem(1), smem(2),
                  pl.BlockSpec((tb, d), lambda i: (i, 0)),
                  pl.BlockSpec((tb, nsel), lambda i: (i, 0)),
                  pl.BlockSpec((tb, d), lambda i: (i, 0)),
                  pl.BlockSpec(memory_space=pl.ANY)],
        out_specs=pl.BlockSpec((tb, d), lambda i: (i, 0)),
        out_shape=jax.ShapeDtypeStruct((t, d), F32),
        scratch_shapes=[pltpu.VMEM((GATHER_SLOTS, tb * nsel * SUBLANES, LANES), I32),
                        pltpu.VMEM((3, d // LANES, nsel, LANES), BF16),
                        pltpu.SemaphoreType.DMA((GATHER_SLOTS,))],
        compiler_params=_params("arbitrary"),
        name="peer_expert",
    )(idx_flat, idx_flat, idx_flat, xn, gate, h, table)


def _ple_kernel(h_ref, g_ref, p_ref, wg_ref, wu_ref, o_ref):
    h = h_ref[...]
    xn = _rms(h, g_ref[...]).astype(BF16)
    gate = jax.nn.sigmoid(jnp.dot(xn, wg_ref[...], preferred_element_type=F32))
    up = jnp.dot(p_ref[...].astype(BF16), wu_ref[...], preferred_element_type=F32)
    o_ref[...] = h + up * gate


def ple(h, gain, p, w_gate, w_up, *, tm=512):
    t, d = h.shape
    pd = p.shape[1]
    return pl.pallas_call(
        _ple_kernel,
        grid=(t // tm,),
        in_specs=[pl.BlockSpec((tm, d), lambda i: (i, 0)),
                  pl.BlockSpec((1, d), lambda i: (0, 0)),
                  pl.BlockSpec((tm, pd), lambda i: (i, 0)),
                  pl.BlockSpec((d, d), lambda i: (0, 0)),
                  pl.BlockSpec((pd, d), lambda i: (0, 0))],
        out_specs=pl.BlockSpec((tm, d), lambda i: (i, 0)),
        out_shape=jax.ShapeDtypeStruct((t, d), F32),
        compiler_params=_params("parallel"),
        name="ple",
    )(h, gain.reshape(1, d), p, w_gate, w_up)


def _mix_even(h, gain, w_in, b_forget, conv_w, q_gain, k_gain, w_out, batch, seq):
    main = 3 * CONV_CH + 3 * FOX_DIM
    w_main = w_in[:, :main].astype(BF16)
    w_f = w_in[:, main:].T.astype(BF16)
    z, f_t = norm_matmul(h, gain, w_main, w_f)
    ya = conv_even(z, conv_w, seq)
    q, k, v = qkv_prep(z, q_gain, k_gain)
    fh, fht = forget_cumsum(f_t, b_forget, batch, seq)
    yb = fox_attention(q, k, v, fh, fht, batch, seq)
    wo = w_out.astype(BF16)
    return out_proj2(h, ya, yb, wo[:CONV_CH], wo[CONV_CH:])


def _mix_odd(h, gain, w_in, conv_w, conv_b, w_a, b_a, w_x, b_x, lam, w_out, batch, seq):
    z = norm_matmul(h, gain, w_in.astype(BF16))
    y = lru_mix(z, conv_w, conv_b, w_a, b_a, w_x, b_x, lam, batch, seq)
    return out_proj1(h, y, w_out.astype(BF16))


def _peer(h, gain, w_query, sub_keys, u, v):
    xn, idx, gate = peer_route(h, gain, w_query.astype(BF16), sub_keys.astype(BF16))
    return peer_expert(h, xn, idx, gate, pack_expert_table(u, v))


def kernel(x, p, norm_mix, norm_ffn, norm_ple, even_w_in, even_b_forget, even_conv_w, even_q_gain,
           even_k_gain, even_w_out, odd_w_in, odd_conv_w, odd_conv_b, odd_w_a, odd_b_a, odd_w_x,
           odd_b_x, odd_lru_param, odd_w_out, peer_w_query, peer_sub_keys, peer_u, peer_v,
           ple_w_up, ple_w_gate):
    batch, seq, d = x.shape
    depth = p.shape[0]
    h = x.reshape(batch * seq, d)
    for layer in range(depth):
        j = layer // 2
        if layer % 2 == 0:
            h = _mix_even(h, norm_mix[layer], even_w_in[j], even_b_forget[j], even_conv_w[j],
                          even_q_gain[j], even_k_gain[j], even_w_out[j], batch, seq)
        else:
            h = _mix_odd(h, norm_mix[layer], odd_w_in[j], odd_conv_w[j], odd_conv_b[j], odd_w_a[j],
                         odd_b_a[j], odd_w_x[j], odd_b_x[j], odd_lru_param[j], odd_w_out[j], batch, seq)
        h = _peer(h, norm_ffn[layer], peer_w_query[layer], peer_sub_keys[layer], peer_u[layer],
                  peer_v[layer])
        h = ple(h, norm_ple[layer], p[layer].reshape(batch * seq, -1), ple_w_gate[layer].astype(BF16),
                ple_w_up[layer].astype(BF16))
    return h.reshape(batch, seq, d)
```

```python
import functools

import jax
import jax.numpy as jnp
from jax import lax
from jax.experimental import pallas as pl
from jax.experimental.pallas import tpu as pltpu

F32 = jnp.float32
BF16 = jnp.bfloat16
I32 = jnp.int32
EPS = 1e-6

LANES = 128
SUBLANES = 8
VMEM_LIMIT = 56 * 1024 * 1024

CONV_CH = 512
CONV_W = 3
FOX_HEADS = 8
FOX_HEAD_DIM = 64
FOX_DIM = FOX_HEADS * FOX_HEAD_DIM
LRU_HEADS = 8
LRU_BLOCK = 128
LRU_CONV_W = 4
LRU_C = 8.0
PEER_HEADS = 8
N_KEYS = 128
PEER_TOPK = 16
HALF = 128
ROUTE_TOKENS = 128
EXPERT_TOKENS = 16
GATHER_SLOTS = 3


def _params(*sem):
    return pltpu.CompilerParams(dimension_semantics=sem, vmem_limit_bytes=VMEM_LIMIT)


def _rms(x, g):
    ms = jnp.mean(x * x, axis=-1, keepdims=True)
    return x * lax.rsqrt(ms + EPS) * g


def _gelu_tanh(x):
    return 0.5 * x * (1.0 + jnp.tanh(0.7978845608028654 * (x + 0.044715 * (x * x * x))))


def _split3(x):
    hi = x.astype(BF16)
    r1 = x - hi.astype(F32)
    mid = r1.astype(BF16)
    lo = (r1 - mid.astype(F32)).astype(BF16)
    return hi, mid, lo


def _norm_mm_kernel(*refs, has_t):
    if has_t:
        x_ref, g_ref, w_ref, wt_ref, o_ref, ot_ref, xn_ref = refs
    else:
        x_ref, g_ref, w_ref, o_ref, xn_ref = refs

    @pl.when(pl.program_id(1) == 0)
    def _():
        xn = _rms(x_ref[...], g_ref[...]).astype(BF16)
        xn_ref[...] = xn
        if has_t:
            ot_ref[...] = lax.dot_general(wt_ref[...], xn, (((1,), (1,)), ((), ())),
                                          preferred_element_type=F32)

    o_ref[...] = jnp.dot(xn_ref[...], w_ref[...], preferred_element_type=F32).astype(o_ref.dtype)


def norm_matmul(x, gain, w, wt=None, *, tm=512, tn=1024, out_dtype=F32):
    t, d = x.shape
    n = w.shape[1]
    tn = min(tn, n)
    in_specs = [pl.BlockSpec((tm, d), lambda i, j: (i, 0)),
                pl.BlockSpec((1, d), lambda i, j: (0, 0)),
                pl.BlockSpec((d, tn), lambda i, j: (0, j))]
    args = [x, gain.reshape(1, d), w]
    out_shape = [jax.ShapeDtypeStruct((t, n), out_dtype)]
    out_specs = [pl.BlockSpec((tm, tn), lambda i, j: (i, j))]
    if wt is not None:
        in_specs.append(pl.BlockSpec(wt.shape, lambda i, j: (0, 0)))
        args.append(wt)
        out_shape.append(jax.ShapeDtypeStruct((wt.shape[0], t), F32))
        out_specs.append(pl.BlockSpec((wt.shape[0], tm), lambda i, j: (0, i)))
    res = pl.pallas_call(
        functools.partial(_norm_mm_kernel, has_t=wt is not None),
        grid=(t // tm, n // tn),
        in_specs=in_specs, out_specs=out_specs, out_shape=out_shape,
        scratch_shapes=[pltpu.VMEM((tm, d), BF16)],
        compiler_params=_params("parallel", "arbitrary"),
        name="norm_matmul_t" if wt is not None else "norm_matmul",
    )(*args)
    return res if wt is not None else res[0]


def _conv_even_kernel(gb_ref, gc_ref, xa_ref, gch_ref, xah_ref, w_ref, o_ref, *, blocks_per_seq):
    i = pl.program_id(0)
    u = gc_ref[...] * xa_ref[...]
    uh = gch_ref[...] * xah_ref[...]
    uh = jnp.where(i % blocks_per_seq == 0, 0.0, uh)
    ext = jnp.concatenate([uh, u], axis=0)
    w = w_ref[...]
    y = (w[2:3] * ext[SUBLANES:]
         + w[1:2] * pltpu.roll(ext, 1, 0)[SUBLANES:]
         + w[0:1] * pltpu.roll(ext, 2, 0)[SUBLANES:])
    o_ref[...] = (gb_ref[...] * y).astype(o_ref.dtype)


def conv_even(z, conv_w, seq, *, ts=512):
    t = z.shape[0]
    c = CONV_CH
    hb = ts // SUBLANES

    def halo(col):
        return pl.BlockSpec((SUBLANES, c), lambda i: (jnp.maximum(i * hb - 1, 0), col))

    return pl.pallas_call(
        functools.partial(_conv_even_kernel, blocks_per_seq=seq // ts),
        grid=(t // ts,),
        in_specs=[pl.BlockSpec((ts, c), lambda i: (i, 0)),
                  pl.BlockSpec((ts, c), lambda i: (i, 1)),
                  pl.BlockSpec((ts, c), lambda i: (i, 2)),
                  halo(1), halo(2),
                  pl.BlockSpec((CONV_W, c), lambda i: (0, 0))],
        out_specs=pl.BlockSpec((ts, c), lambda i: (i, 0)),
        out_shape=jax.ShapeDtypeStruct((t, c), BF16),
        compiler_params=_params("parallel"),
        name="conv_even",
    )(z, z, z, z, z, conv_w)


def _head_norm(x, gain, bd):
    x2 = x * x
    hi = x2.astype(BF16)
    lo = (x2 - hi.astype(F32)).astype(BF16)
    ss = jnp.dot(hi, bd, preferred_element_type=F32) + jnp.dot(lo, bd, preferred_element_type=F32)
    return x * lax.rsqrt(ss * (1.0 / FOX_HEAD_DIM) + EPS) * gain


def _qkv_prep_kernel(q_ref, k_ref, v_ref, qg_ref, kg_ref, bd_ref, qo_ref, ko_ref, vo_ref):
    bd = bd_ref[...]
    scale = FOX_HEAD_DIM ** -0.5
    qo_ref[...] = (_head_norm(q_ref[...], qg_ref[...], bd) * scale).astype(BF16)
    ko_ref[...] = _head_norm(k_ref[...], kg_ref[...], bd).astype(BF16)
    vo_ref[...] = v_ref[...].astype(BF16)


def qkv_prep(z, q_gain, k_gain, *, ts=512):
    t = z.shape[0]
    c = FOX_DIM
    qg = jnp.tile(q_gain, FOX_HEADS).reshape(1, c)
    kg = jnp.tile(k_gain, FOX_HEADS).reshape(1, c)
    head = jnp.arange(c) // FOX_HEAD_DIM
    bd = (head[:, None] == head[None, :]).astype(BF16)
    blk = lambda col: pl.BlockSpec((ts, c), lambda i: (i, col))
    one = lambda shape: pl.BlockSpec(shape, lambda i: (0, 0))
    return pl.pallas_call(
        _qkv_prep_kernel,
        grid=(t // ts,),
        in_specs=[blk(3), blk(4), blk(5), one((1, c)), one((1, c)), one((c, c))],
        out_specs=[pl.BlockSpec((ts, c), lambda i: (i, 0))] * 3,
        out_shape=[jax.ShapeDtypeStruct((t, c), BF16)] * 3,
        compiler_params=_params("parallel"),
        name="qkv_prep",
    )(z, z, z, qg, kg, bd)


def _forget_cumsum_kernel(f_ref, b_ref, o_ref, ot_ref, *, seq):
    r = lax.broadcasted_iota(I32, (LANES, LANES), 0)
    c = lax.broadcasted_iota(I32, (LANES, LANES), 1)
    upper = (r <= c).astype(BF16)
    eye = (r == c).astype(BF16)
    b = b_ref[...]
    carry = jnp.zeros((FOX_HEADS, 1), F32)
    for ch in range(seq // LANES):
        x = f_ref[:, ch * LANES:(ch + 1) * LANES] + b
        ls = jnp.minimum(x, 0.0) - jnp.log1p(jnp.exp(-jnp.abs(x)))
        pre = sum(jnp.dot(p, upper, preferred_element_type=F32) for p in _split3(ls))
        out = pre + carry
        o_ref[:, ch * LANES:(ch + 1) * LANES] = out
        carry = out[:, LANES - 1:LANES]
        ot_ref[ch * LANES:(ch + 1) * LANES, :] = sum(
            lax.dot_general(eye, p, (((1,), (1,)), ((), ())), preferred_element_type=F32)
            for p in _split3(out))


def forget_cumsum(f_t, b_forget, batch, seq):
    return pl.pallas_call(
        functools.partial(_forget_cumsum_kernel, seq=seq),
        grid=(batch,),
        in_specs=[pl.BlockSpec((FOX_HEADS, seq), lambda b: (0, b)),
                  pl.BlockSpec((FOX_HEADS, 1), lambda b: (0, 0))],
        out_specs=[pl.BlockSpec((None, FOX_HEADS, seq), lambda b: (b, 0, 0)),
                   pl.BlockSpec((seq, FOX_HEADS), lambda b: (b, 0))],
        out_shape=[jax.ShapeDtypeStruct((batch, FOX_HEADS, seq), F32),
                   jax.ShapeDtypeStruct((batch * seq, FOX_HEADS), F32)],
        compiler_params=_params("parallel"),
        name="forget_cumsum",
    )(f_t, b_forget.reshape(FOX_HEADS, 1))


def _fox_attn_kernel(q_ref, k_ref, v_ref, fh_ref, fht_ref, o_ref, *, tq):
    pair = pl.program_id(1)
    i = pl.program_id(2)
    q = q_ref[...]
    lane = lax.broadcasted_iota(I32, (tq, LANES), 1)
    lane8 = lax.broadcasted_iota(I32, (tq, FOX_HEADS), 1)
    row = lax.broadcasted_iota(I32, (tq, tq), 0)
    col = lax.broadcasted_iota(I32, (tq, tq), 1)
    fht = fht_ref[...]
    heads = []
    for hh in range(2):
        first = lane < FOX_HEAD_DIM
        qm = jnp.where(first if hh == 0 else jnp.logical_not(first), q, jnp.zeros_like(q))
        fq = jnp.sum(jnp.where(lane8 == 2 * pair + hh, fht, 0.0), axis=-1, keepdims=True)

        def kv_step(j, carry, masked):
            m, l, acc = carry
            start = pl.multiple_of(j * tq, tq)
            kb = k_ref[pl.ds(start, tq), :]
            vb = v_ref[pl.ds(start, tq), :]
            s = lax.dot_general(qm, kb, (((1,), (1,)), ((), ())), preferred_element_type=F32)
            t = s - fh_ref[hh:hh + 1, pl.ds(start, tq)]
            if masked:
                t = jnp.where(col <= row, t, -jnp.inf)
            m_new = jnp.maximum(m, jnp.max(t, axis=-1, keepdims=True) + fq)
            p = jnp.exp(t + (fq - m_new))
            alpha = jnp.exp(m - m_new)
            l = alpha * l + jnp.sum(p, axis=-1, keepdims=True)
            acc = alpha * acc + jnp.dot(p.astype(BF16), vb, preferred_element_type=F32)
            return m_new, l, acc

        init = (jnp.full((tq, 1), -jnp.inf, F32), jnp.zeros((tq, 1), F32), jnp.zeros((tq, LANES), F32))
        carry = lax.fori_loop(0, i, functools.partial(kv_step, masked=False), init)
        _, l, acc = kv_step(i, carry, True)
        heads.append(acc / l)
    o_ref[...] = jnp.where(lane < FOX_HEAD_DIM, heads[0], heads[1]).astype(o_ref.dtype)


def fox_attention(q, k, v, fh, fht, batch, seq, *, tq=1024):
    t = q.shape[0]
    nq = seq // tq
    pairs = FOX_HEADS // 2
    fh4 = fh.reshape(batch, pairs, 2, seq)
    return pl.pallas_call(
        functools.partial(_fox_attn_kernel, tq=tq),
        grid=(batch, pairs, nq),
        in_specs=[pl.BlockSpec((tq, LANES), lambda b, p, i: (b * nq + i, p)),
                  pl.BlockSpec((seq, LANES), lambda b, p, i: (b, p)),
                  pl.BlockSpec((seq, LANES), lambda b, p, i: (b, p)),
                  pl.BlockSpec((None, None, 2, seq), lambda b, p, i: (b, p, 0, 0)),
                  pl.BlockSpec((tq, FOX_HEADS), lambda b, p, i: (b * nq + i, 0))],
        out_specs=pl.BlockSpec((tq, LANES), lambda b, p, i: (b * nq + i, p)),
        out_shape=jax.ShapeDtypeStruct((t, FOX_DIM), BF16),
        compiler_params=_params("parallel", "parallel", "arbitrary"),
        name="fox_attention",
    )(q, k, v, fh4, fht)


def _out_proj2_kernel(h_ref, a_ref, b_ref, wa_ref, wb_ref, o_ref):
    o_ref[...] = (h_ref[...]
                  + jnp.dot(a_ref[...], wa_ref[...], preferred_element_type=F32)
                  + jnp.dot(b_ref[...], wb_ref[...], preferred_element_type=F32))


def out_proj2(h, a, b, wa, wb, *, tm=512):
    t, d = h.shape
    ka, kb = a.shape[1], b.shape[1]
    return pl.pallas_call(
        _out_proj2_kernel,
        grid=(t // tm,),
        in_specs=[pl.BlockSpec((tm, d), lambda i: (i, 0)),
                  pl.BlockSpec((tm, ka), lambda i: (i, 0)),
                  pl.BlockSpec((tm, kb), lambda i: (i, 0)),
                  pl.BlockSpec((ka, d), lambda i: (0, 0)),
                  pl.BlockSpec((kb, d), lambda i: (0, 0))],
        out_specs=pl.BlockSpec((tm, d), lambda i: (i, 0)),
        out_shape=jax.ShapeDtypeStruct((t, d), F32),
        compiler_params=_params("parallel"),
        name="out_proj2",
    )(h, a, b, wa, wb)


def _out_proj1_kernel(h_ref, a_ref, w_ref, o_ref):
    o_ref[...] = h_ref[...] + jnp.dot(a_ref[...], w_ref[...], preferred_element_type=F32)


def out_proj1(h, a, w, *, tm=512):
    t, d = h.shape
    ka = a.shape[1]
    return pl.pallas_call(
        _out_proj1_kernel,
        grid=(t // tm,),
        in_specs=[pl.BlockSpec((tm, d), lambda i: (i, 0)),
                  pl.BlockSpec((tm, ka), lambda i: (i, 0)),
                  pl.BlockSpec((ka, d), lambda i: (0, 0))],
        out_specs=pl.BlockSpec((tm, d), lambda i: (i, 0)),
        out_shape=jax.ShapeDtypeStruct((t, d), F32),
        compiler_params=_params("parallel"),
        name="out_proj1",
    )(h, a, w)


def _shift_rows(x, s, fill):
    rows = lax.broadcasted_iota(I32, x.shape, 0)
    return jnp.where(rows < s, fill, pltpu.roll(x, s, 0))


def _lru_kernel(gate_ref, xr_ref, cw_ref, cb_ref, wa_ref, ba_ref, wx_ref, bx_ref, lam_ref,
                o_ref, halo_ref, carry_ref, *, ts):
    @pl.when(pl.program_id(1) == 0)
    def _():
        halo_ref[...] = jnp.zeros_like(halo_ref)
        carry_ref[...] = jnp.zeros_like(carry_ref)

    x = xr_ref[...]
    ext = jnp.concatenate([halo_ref[...], x], axis=0)
    halo_ref[...] = x[ts - SUBLANES:, :]
    cw = cw_ref[...]
    xc = cb_ref[...] + cw[3:4] * x
    for k in range(LRU_CONV_W - 1):
        xc = xc + cw[k:k + 1] * pltpu.roll(ext, LRU_CONV_W - 1 - k, 0)[SUBLANES:]

    xcb = xc.astype(BF16)
    ra, ri = [], []
    for h in range(LRU_HEADS):
        xh = xcb[:, h * LRU_BLOCK:(h + 1) * LRU_BLOCK]
        ra.append(jnp.dot(xh, wa_ref[h], preferred_element_type=F32))
        ri.append(jnp.dot(xh, wx_ref[h], preferred_element_type=F32))
    r = jax.nn.sigmoid(jnp.concatenate(ra, axis=1) + ba_ref[...])
    gi = jax.nn.sigmoid(jnp.concatenate(ri, axis=1) + bx_ref[...])
    nl = -lam_ref[...]
    softplus = jnp.maximum(nl, 0.0) + jnp.log1p(jnp.exp(-jnp.abs(nl)))
    log_a = -LRU_C * r * softplus
    a = jnp.exp(log_a)
    mult = jnp.sqrt(-jnp.tanh(log_a) * (a * a + 1.0))
    u = mult * (gi * xc)

    s = 1
    while s < ts:
        u = a * _shift_rows(u, s, 0.0) + u
        a = a * _shift_rows(a, s, 1.0)
        s *= 2
    hs = u + a * carry_ref[0:1, :]
    carry_ref[...] = jnp.broadcast_to(hs[ts - 1:ts, :], carry_ref.shape)
    o_ref[...] = (hs * _gelu_tanh(gate_ref[...])).astype(o_ref.dtype)


def lru_mix(z, conv_w, conv_b, w_a, b_a, w_x, b_x, lam, batch, seq, *, ts=256):
    t = z.shape[0]
    w = z.shape[1] // 2
    nb = seq // ts
    row = lambda: pl.BlockSpec((1, w), lambda b, i: (0, 0))
    blockw = lambda: pl.BlockSpec((LRU_HEADS, LRU_BLOCK, LRU_BLOCK), lambda b, i: (0, 0, 0))
    return pl.pallas_call(
        functools.partial(_lru_kernel, ts=ts),
        grid=(batch, nb),
        in_specs=[pl.BlockSpec((ts, w), lambda b, i: (b * nb + i, 0)),
                  pl.BlockSpec((ts, w), lambda b, i: (b * nb + i, 1)),
                  pl.BlockSpec((LRU_CONV_W, w), lambda b, i: (0, 0)),
                  row(), blockw(), row(), blockw(), row(), row()],
        out_specs=pl.BlockSpec((ts, w), lambda b, i: (b * nb + i, 0)),
        out_shape=jax.ShapeDtypeStruct((t, w), BF16),
        scratch_shapes=[pltpu.VMEM((SUBLANES, w), F32), pltpu.VMEM((SUBLANES, w), F32)],
        compiler_params=_params("parallel", "arbitrary"),
        name="lru_mix",
    )(z, z, conv_w, conv_b.reshape(1, w), w_a.astype(BF16), b_a.reshape(1, w),
      w_x.astype(BF16), b_x.reshape(1, w), lam.reshape(1, w))


def _top_rows(vals, order, payload, k):
    nlane = vals.shape[1]
    kio = lax.broadcasted_iota(I32, (k, nlane), 0)
    top_v = jnp.zeros((k, nlane), F32)
    top_p = jnp.zeros((k, nlane), F32)
    for r in range(k):
        m = jnp.max(vals, axis=0, keepdims=True)
        first = jnp.min(jnp.where(vals == m, order, jnp.inf), axis=0, keepdims=True)
        sel = order == first
        if payload is None:
            p = first
        else:
            p = jnp.sum(jnp.where(sel, payload, 0.0), axis=0, keepdims=True)
        vals = jnp.where(sel, -jnp.inf, vals)
        top_v = jnp.where(kio == r, m, top_v)
        top_p = jnp.where(kio == r, p, top_p)
    return top_v, top_p


def _pair_candidates(s1, i1, s2, i2):
    k = PEER_TOPK
    half = k // 2
    sub = lax.broadcasted_iota(I32, (half, s1.shape[1]), 0).astype(F32)
    vals, flat, expert = [], [], []
    for a in range(half):
        vals.append(s1[a:a + 1, :] + s2[:half])
        flat.append(sub + float(a * k))
        expert.append(i1[a:a + 1, :] * float(N_KEYS) + i2[:half])
    vals.append(s1[0:1, :] + s2[half:])
    flat.append(sub + float(half))
    expert.append(i1[0:1, :] * float(N_KEYS) + i2[half:])
    vals.append(s1[half:] + s2[0:1, :])
    flat.append((sub + float(half)) * float(k))
    expert.append(i1[half:] * float(N_KEYS) + i2[0:1, :])
    return (jnp.concatenate(vals, axis=0), jnp.concatenate(flat, axis=0),
            jnp.concatenate(expert, axis=0))


def _peer_route_kernel(h_ref, g_ref, wq_ref, keys_ref, xn_ref, idx_ref, gate_ref):
    xn = _rms(h_ref[...], g_ref[...]).astype(BF16)
    xn_ref[...] = xn
    q = jnp.dot(xn, wq_ref[...], preferred_element_type=F32).astype(BF16)
    k = PEER_TOPK
    key_id = lax.broadcasted_iota(I32, (N_KEYS, q.shape[0]), 0).astype(F32)
    idx_rows, gate_rows = [], []
    for h in range(PEER_HEADS):
        tops = []
        for c in range(2):
            qhc = q[:, (2 * h + c) * HALF:(2 * h + c + 1) * HALF]
            sc = lax.dot_general(keys_ref[c], qhc, (((1,), (1,)), ((), ())), preferred_element_type=F32)
            tops.append(_top_rows(sc, key_id, None, k))
        (s1, i1), (s2, i2) = tops
        cand, flat, expert = _pair_candidates(s1, i1, s2, i2)
        best_s, best_e = _top_rows(cand, flat, expert, k)
        e = jnp.exp(best_s - jnp.max(best_s, axis=0, keepdims=True))
        gate_rows.append(e / jnp.sum(e, axis=0, keepdims=True))
        idx_rows.append(best_e)
    idx_t = jnp.concatenate(idx_rows, axis=0)
    gate_t = jnp.concatenate(gate_rows, axis=0)
    idx_ref[...] = idx_t.T.astype(I32)
    gate_ref[...] = gate_t.T


def peer_route(h, gain, wq, keys):
    t, d = h.shape
    tb = ROUTE_TOKENS
    nq = wq.shape[1]
    return pl.pallas_call(
        _peer_route_kernel,
        grid=(t // tb,),
        in_specs=[pl.BlockSpec((tb, d), lambda i: (i, 0)),
                  pl.BlockSpec((1, d), lambda i: (0, 0)),
                  pl.BlockSpec((d, nq), lambda i: (0, 0)),
                  pl.BlockSpec((2, N_KEYS, HALF), lambda i: (0, 0, 0))],
        out_specs=[pl.BlockSpec((tb, d), lambda i: (i, 0)),
                   pl.BlockSpec((tb, PEER_HEADS * PEER_TOPK), lambda i: (i, 0)),
                   pl.BlockSpec((tb, PEER_HEADS * PEER_TOPK), lambda i: (i, 0))],
        out_shape=[jax.ShapeDtypeStruct((t, d), BF16),
                   jax.ShapeDtypeStruct((t, PEER_HEADS * PEER_TOPK), I32),
                   jax.ShapeDtypeStruct((t, PEER_HEADS * PEER_TOPK), F32)],
        compiler_params=_params("parallel"),
        name="peer_route",
    )(h, gain.reshape(1, d), wq, keys)


def _peer_expert_kernel(idx0, idx1, idx2, xn_ref, gate_ref, h_ref, tab_ref, o_ref, buf, vstash, sem,
                        *, tb, nblk, d):
    i = pl.program_id(0)
    slot = i % GATHER_SLOTS
    ahead = (i + 2) % GATHER_SLOTS
    nsel = PEER_HEADS * PEER_TOPK
    tile = SUBLANES
    nchunk = d // LANES
    per_tok = nsel * tile
    per_chunk = nsel // nchunk

    def tile_copy(e, s, r):
        return pltpu.make_async_copy(
            tab_ref.at[pl.ds(pl.multiple_of(e * tile, tile), tile)],
            buf.at[s, pl.ds(pl.multiple_of(r * tile, tile), tile)], sem.at[s])

    def slot_wait(s):
        pltpu.make_async_copy(tab_ref.at[pl.ds(0, tb * per_tok)], buf.at[s], sem.at[s]).wait()

    @pl.when(i == 0)
    def _():
        def prime(r, c):
            tile_copy(idx0[r], 0, r).start()
            tile_copy(idx1[r], 1, r).start()
            return c
        lax.fori_loop(0, tb * nsel, prime, 0)

    slot_wait(slot)

    def chunk_words(t, c):
        return buf[slot, pl.ds(t * per_tok + c, nsel, stride=tile), :]

    acts, ws = {}, {}
    for t in range(tb + 2):
        if t < tb:
            x8 = jnp.broadcast_to(xn_ref[t:t + 1, :], (SUBLANES, d))
            act = jnp.zeros((SUBLANES, nsel), F32)
        for c in range(nchunk):
            if t < tb:
                for n in range(c * per_chunk, (c + 1) * per_chunk):
                    tile_copy(idx2[t * nsel + n], ahead, t * nsel + n).start(priority=n % 2)
                words = chunk_words(t, c)
                u_c = lax.bitcast_convert_type(words & jnp.int32(-65536), F32).astype(BF16)
                vstash[t % 3, c] = lax.bitcast_convert_type(words << 16, F32).astype(BF16)
                act = act + lax.dot_general(x8[:, c * LANES:(c + 1) * LANES], u_c, (((1,), (1,)), ((), ())),
                                            preferred_element_type=F32)
            if t >= 2:
                out = jnp.dot(ws[t - 2], vstash[(t - 2) % 3, c], preferred_element_type=F32)
                cols = slice(c * LANES, (c + 1) * LANES)
                o_ref[t - 2:t - 1, cols] = h_ref[t - 2:t - 1, cols] + out[0:1, :]
        if t < tb:
            acts[t] = act
        if 1 <= t <= tb:
            ws[t - 1] = (gate_ref[t - 1:t, :] * _gelu_tanh(acts.pop(t - 1))).astype(BF16)

    @pl.when(i == nblk - 1)
    def _():
        slot_wait(nblk % GATHER_SLOTS)
        slot_wait((nblk + 1) % GATHER_SLOTS)


def pack_expert_table(u, v):
    ne, d = u.shape
    ub = lax.bitcast_convert_type(u.astype(BF16), jnp.uint16).astype(jnp.uint32)
    vb = lax.bitcast_convert_type(v.astype(BF16), jnp.uint16).astype(jnp.uint32)
    words = lax.bitcast_convert_type((ub << 16) | vb, I32)
    return words.reshape(ne * (d // LANES), LANES)


def peer_expert(h, xn, idx, gate, table):
    t, d = h.shape
    assert d == SUBLANES * LANES
    tb = EXPERT_TOKENS
    nsel = PEER_HEADS * PEER_TOPK
    nblk = t // tb
    assert nblk >= 2
    idx_flat = idx.reshape(t * nsel)
    smem = lambda k: pl.BlockSpec((tb * nsel,), lambda i: (jnp.minimum(i + k, nblk - 1),),
                                  memory_space=pltpu.SMEM)
    return pl.pallas_call(
        functools.partial(_peer_expert_kernel, tb=tb, nblk=nblk, d=d),
        grid=(nblk,),
        in_specs=[smem(0), smem(1), smem(2),
                  pl.BlockSpec((tb, d), lambda i: (i, 0)),
                  pl.BlockSpec((tb, nsel), lambda i: (i, 0)),
                  pl.BlockSpec((tb, d), lambda i: (i, 0)),
                  pl.BlockSpec(memory_space=pl.ANY)],
        out_specs=pl.BlockSpec((tb, d), lambda i: (i, 0)),
        out_shape=jax.ShapeDtypeStruct((t, d), F32),
        scratch_shapes=[pltpu.VMEM((GATHER_SLOTS, tb * nsel * SUBLANES, LANES), I32),
                        pltpu.VMEM((3, d // LANES, nsel, LANES), BF16),
                        pltpu.SemaphoreType.DMA((GATHER_SLOTS,))],
        compiler_params=_params("arbitrary"),
        name="peer_expert",
    )(idx_flat, idx_flat, idx_flat, xn, gate, h, table)


def _ple_kernel(h_ref, g_ref, p_ref, wg_ref, wu_ref, o_ref):
    h = h_ref[...]
    xn = _rms(h, g_ref[...]).astype(BF16)
    gate = jax.nn.sigmoid(jnp.dot(xn, wg_ref[...], preferred_element_type=F32))
    up = jnp.dot(p_ref[...].astype(BF16), wu_ref[...], preferred_element_type=F32)
    o_ref[...] = h + up * gate


def ple(h, gain, p, w_gate, w_up, *, tm=512):
    t, d = h.shape
    pd = p.shape[1]
    return pl.pallas_call(
        _ple_kernel,
        grid=(t // tm,),
        in_specs=[pl.BlockSpec((tm, d), lambda i: (i, 0)),
                  pl.BlockSpec((1, d), lambda i: (0, 0)),
                  pl.BlockSpec((tm, pd), lambda i: (i, 0)),
                  pl.BlockSpec((d, d), lambda i: (0, 0)),
                  pl.BlockSpec((pd, d), lambda i: (0, 0))],
        out_specs=pl.BlockSpec((tm, d), lambda i: (i, 0)),
        out_shape=jax.ShapeDtypeStruct((t, d), F32),
        compiler_params=_params("parallel"),
        name="ple",
    )(h, gain.reshape(1, d), p, w_gate, w_up)


def _mix_even(h, gain, w_in, b_forget, conv_w, q_gain, k_gain, w_out, batch, seq):
    main = 3 * CONV_CH + 3 * FOX_DIM
    w_main = w_in[:, :main].astype(BF16)
    w_f = w_in[:, main:].T.astype(BF16)
    z, f_t = norm_matmul(h, gain, w_main, w_f)
    ya = conv_even(z, conv_w, seq)
    q, k, v = qkv_prep(z, q_gain, k_gain)
    fh, fht = forget_cumsum(f_t, b_forget, batch, seq)
    yb = fox_attention(q, k, v, fh, fht, batch, seq)
    wo = w_out.astype(BF16)
    return out_proj2(h, ya, yb, wo[:CONV_CH], wo[CONV_CH:])


def _mix_odd(h, gain, w_in, conv_w, conv_b, w_a, b_a, w_x, b_x, lam, w_out, batch, seq):
    z = norm_matmul(h, gain, w_in.astype(BF16))
    y = lru_mix(z, conv_w, conv_b, w_a, b_a, w_x, b_x, lam, batch, seq)
    return out_proj1(h, y, w_out.astype(BF16))


def _peer(h, gain, w_query, sub_keys, u, v):
    xn, idx, gate = peer_route(h, gain, w_query.astype(BF16), sub_keys.astype(BF16))
    return peer_expert(h, xn, idx, gate, pack_expert_table(u, v))


def kernel(x, p, norm_mix, norm_ffn, norm_ple, even_w_in, even_b_forget, even_conv_w, even_q_gain,
           even_k_gain, even_w_out, odd_w_in, odd_conv_w, odd_conv_b, odd_w_a, odd_b_a, odd_w_x,
           odd_b_x, odd_lru_param, odd_w_out, peer_w_query, peer_sub_keys, peer_u, peer_v,
           ple_w_up, ple_w_gate):
    batch, seq, d = x.shape
    depth = p.shape[0]
    h = x.reshape(batch * seq, d)
    for layer in range(depth):
        j = layer // 2
        if layer % 2 == 0:
            h = _mix_even(h, norm_mix[layer], even_w_in[j], even_b_forget[j], even_conv_w[j],
                          even_q_gain[j], even_k_gain[j], even_w_out[j], batch, seq)
        else:
            h = _mix_odd(h, norm_mix[layer], odd_w_in[j], odd_conv_w[j], odd_conv_b[j], odd_w_a[j],
                         odd_b_a[j], odd_w_x[j], odd_b_x[j], odd_lru_param[j], odd_w_out[j], batch, seq)
        h = _peer(h, norm_ffn[layer], peer_w_query[layer], peer_sub_keys[layer], peer_u[layer],
                  peer_v[layer])
        h = ple(h, norm_ple[layer], p[layer].reshape(batch * seq, -1), ple_w_gate[layer].astype(BF16),
                ple_w_up[layer].astype(BF16))
    return h.reshape(batch, seq, d)
```

```python
import functools

import jax
import jax.numpy as jnp
from jax import lax
from jax.experimental import pallas as pl
from jax.experimental.pallas import tpu as pltpu

F32 = jnp.float32
BF16 = jnp.bfloat16
I32 = jnp.int32
EPS = 1e-6

LANES = 128
SUBLANES = 8
VMEM_LIMIT = 56 * 1024 * 1024

CONV_CH = 512
CONV_W = 3
FOX_HEADS = 8
FOX_HEAD_DIM = 64
FOX_DIM = FOX_HEADS * FOX_HEAD_DIM
LRU_HEADS = 8
LRU_BLOCK = 128
LRU_CONV_W = 4
LRU_C = 8.0
PEER_HEADS = 8
N_KEYS = 128
PEER_TOPK = 16
HALF = 128
ROUTE_TOKENS = 128
EXPERT_TOKENS = 16
GATHER_SLOTS = 3


def _params(*sem):
    return pltpu.CompilerParams(dimension_semantics=sem, vmem_limit_bytes=VMEM_LIMIT)


def _rms(x, g):
    ms = jnp.mean(x * x, axis=-1, keepdims=True)
    return x * lax.rsqrt(ms + EPS) * g


def _gelu_tanh(x):
    return 0.5 * x * (1.0 + jnp.tanh(0.7978845608028654 * (x + 0.044715 * (x * x * x))))


def _split3(x):
    hi = x.astype(BF16)
    r1 = x - hi.astype(F32)
    mid = r1.astype(BF16)
    lo = (r1 - mid.astype(F32)).astype(BF16)
    return hi, mid, lo


def _norm_mm_kernel(*refs, has_t):
    if has_t:
        x_ref, g_ref, w_ref, wt_ref, o_ref, ot_ref, xn_ref = refs
    else:
        x_ref, g_ref, w_ref, o_ref, xn_ref = refs

    @pl.when(pl.program_id(1) == 0)
    def _():
        xn = _rms(x_ref[...], g_ref[...]).astype(BF16)
        xn_ref[...] = xn
        if has_t:
            ot_ref[...] = lax.dot_general(wt_ref[...], xn, (((1,), (1,)), ((), ())),
                                          preferred_element_type=F32)

    o_ref[...] = jnp.dot(xn_ref[...], w_ref[...], preferred_element_type=F32).astype(o_ref.dtype)


def norm_matmul(x, gain, w, wt=None, *, tm=1024, tn=1024, out_dtype=F32):
    t, d = x.shape
    n = w.shape[1]
    tn = min(tn, n)
    in_specs = [pl.BlockSpec((tm, d), lambda i, j: (i, 0)),
                pl.BlockSpec((1, d), lambda i, j: (0, 0)),
                pl.BlockSpec((d, tn), lambda i, j: (0, j))]
    args = [x, gain.reshape(1, d), w]
    out_shape = [jax.ShapeDtypeStruct((t, n), out_dtype)]
    out_specs = [pl.BlockSpec((tm, tn), lambda i, j: (i, j))]
    if wt is not None:
        in_specs.append(pl.BlockSpec(wt.shape, lambda i, j: (0, 0)))
        args.append(wt)
        out_shape.append(jax.ShapeDtypeStruct((wt.shape[0], t), F32))
        out_specs.append(pl.BlockSpec((wt.shape[0], tm), lambda i, j: (0, i)))
    res = pl.pallas_call(
        functools.partial(_norm_mm_kernel, has_t=wt is not None),
        grid=(t // tm, n // tn),
        in_specs=in_specs, out_specs=out_specs, out_shape=out_shape,
        scratch_shapes=[pltpu.VMEM((tm, d), BF16)],
        compiler_params=_params("parallel", "arbitrary"),
        name="norm_matmul_t" if wt is not None else "norm_matmul",
    )(*args)
    return res if wt is not None else res[0]


def _conv_even_kernel(gb_ref, gc_ref, xa_ref, gch_ref, xah_ref, w_ref, o_ref, *, blocks_per_seq):
    i = pl.program_id(0)
    u = gc_ref[...] * xa_ref[...]
    uh = gch_ref[...] * xah_ref[...]
    uh = jnp.where(i % blocks_per_seq == 0, 0.0, uh)
    ext = jnp.concatenate([uh, u], axis=0)
    w = w_ref[...]
    y = (w[2:3] * ext[SUBLANES:]
         + w[1:2] * pltpu.roll(ext, 1, 0)[SUBLANES:]
         + w[0:1] * pltpu.roll(ext, 2, 0)[SUBLANES:])
    o_ref[...] = (gb_ref[...] * y).astype(o_ref.dtype)


def conv_even(z, conv_w, seq, *, ts=1024):
    t = z.shape[0]
    c = CONV_CH
    hb = ts // SUBLANES

    def halo(col):
        return pl.BlockSpec((SUBLANES, c), lambda i: (jnp.maximum(i * hb - 1, 0), col))

    return pl.pallas_call(
        functools.partial(_conv_even_kernel, blocks_per_seq=seq // ts),
        grid=(t // ts,),
        in_specs=[pl.BlockSpec((ts, c), lambda i: (i, 0)),
                  pl.BlockSpec((ts, c), lambda i: (i, 1)),
                  pl.BlockSpec((ts, c), lambda i: (i, 2)),
                  halo(1), halo(2),
                  pl.BlockSpec((CONV_W, c), lambda i: (0, 0))],
        out_specs=pl.BlockSpec((ts, c), lambda i: (i, 0)),
        out_shape=jax.ShapeDtypeStruct((t, c), BF16),
        compiler_params=_params("parallel"),
        name="conv_even",
    )(z, z, z, z, z, conv_w)


def _head_norm(x, gain, bd):
    x2 = x * x
    hi = x2.astype(BF16)
    lo = (x2 - hi.astype(F32)).astype(BF16)
    ss = jnp.dot(hi, bd, preferred_element_type=F32) + jnp.dot(lo, bd, preferred_element_type=F32)
    return x * lax.rsqrt(ss * (1.0 / FOX_HEAD_DIM) + EPS) * gain


def _qkv_prep_kernel(q_ref, k_ref, v_ref, qg_ref, kg_ref, bd_ref, qo_ref, ko_ref, vo_ref):
    bd = bd_ref[...]
    scale = FOX_HEAD_DIM ** -0.5
    qo_ref[...] = (_head_norm(q_ref[...], qg_ref[...], bd) * scale).astype(BF16)
    ko_ref[...] = _head_norm(k_ref[...], kg_ref[...], bd).astype(BF16)
    vo_ref[...] = v_ref[...].astype(BF16)


def qkv_prep(z, q_gain, k_gain, *, ts=1024):
    t = z.shape[0]
    c = FOX_DIM
    qg = jnp.tile(q_gain, FOX_HEADS).reshape(1, c)
    kg = jnp.tile(k_gain, FOX_HEADS).reshape(1, c)
    head = jnp.arange(c) // FOX_HEAD_DIM
    bd = (head[:, None] == head[None, :]).astype(BF16)
    blk = lambda col: pl.BlockSpec((ts, c), lambda i: (i, col))
    one = lambda shape: pl.BlockSpec(shape, lambda i: (0, 0))
    return pl.pallas_call(
        _qkv_prep_kernel,
        grid=(t // ts,),
        in_specs=[blk(3), blk(4), blk(5), one((1, c)), one((1, c)), one((c, c))],
        out_specs=[pl.BlockSpec((ts, c), lambda i: (i, 0))] * 3,
        out_shape=[jax.ShapeDtypeStruct((t, c), BF16)] * 3,
        compiler_params=_params("parallel"),
        name="qkv_prep",
    )(z, z, z, qg, kg, bd)


def _forget_cumsum_kernel(f_ref, b_ref, o_ref, ot_ref, *, seq):
    r = lax.broadcasted_iota(I32, (LANES, LANES), 0)
    c = lax.broadcasted_iota(I32, (LANES, LANES), 1)
    upper = (r <= c).astype(BF16)
    eye = (r == c).astype(BF16)
    b = b_ref[...]
    carry = jnp.zeros((FOX_HEADS, 1), F32)
    for ch in range(seq // LANES):
        x = f_ref[:, ch * LANES:(ch + 1) * LANES] + b
        ls = jnp.minimum(x, 0.0) - jnp.log1p(jnp.exp(-jnp.abs(x)))
        pre = sum(jnp.dot(p, upper, preferred_element_type=F32) for p in _split3(ls))
        out = pre + carry
        o_ref[:, ch * LANES:(ch + 1) * LANES] = out
        carry = out[:, LANES - 1:LANES]
        ot_ref[ch * LANES:(ch + 1) * LANES, :] = sum(
            lax.dot_general(eye, p, (((1,), (1,)), ((), ())), preferred_element_type=F32)
            for p in _split3(out))


def forget_cumsum(f_t, b_forget, batch, seq):
    return pl.pallas_call(
        functools.partial(_forget_cumsum_kernel, seq=seq),
        grid=(batch,),
        in_specs=[pl.BlockSpec((FOX_HEADS, seq), lambda b: (0, b)),
                  pl.BlockSpec((FOX_HEADS, 1), lambda b: (0, 0))],
        out_specs=[pl.BlockSpec((None, FOX_HEADS, seq), lambda b: (b, 0, 0)),
                   pl.BlockSpec((seq, FOX_HEADS), lambda b: (b, 0))],
        out_shape=[jax.ShapeDtypeStruct((batch, FOX_HEADS, seq), F32),
                   jax.ShapeDtypeStruct((batch * seq, FOX_HEADS), F32)],
        compiler_params=_params("parallel"),
        name="forget_cumsum",
    )(f_t, b_forget.reshape(FOX_HEADS, 1))


def _fox_attn_kernel(q_ref, k_ref, v_ref, fh_ref, fht_ref, o_ref, *, tq):
    pair = pl.program_id(1)
    i = pl.program_id(2)
    q = q_ref[...]
    lane = lax.broadcasted_iota(I32, (tq, LANES), 1)
    lane8 = lax.broadcasted_iota(I32, (tq, FOX_HEADS), 1)
    row = lax.broadcasted_iota(I32, (tq, tq), 0)
    col = lax.broadcasted_iota(I32, (tq, tq), 1)
    fht = fht_ref[...]
    first = lane < FOX_HEAD_DIM
    qms = [jnp.where(first, q, jnp.zeros_like(q)), jnp.where(first, jnp.zeros_like(q), q)]
    fqs = [jnp.sum(jnp.where(lane8 == 2 * pair + hh, fht, 0.0), axis=-1, keepdims=True)
           for hh in range(2)]

    def kv_step(j, carry, masked):
        start = pl.multiple_of(j * tq, tq)
        kb = k_ref[pl.ds(start, tq), :]
        vb = v_ref[pl.ds(start, tq), :]
        out = []
        for hh in range(2):
            m, l, acc = carry[hh]
            fq = fqs[hh]
            s = lax.dot_general(qms[hh], kb, (((1,), (1,)), ((), ())), preferred_element_type=F32)
            t = s - fh_ref[hh:hh + 1, pl.ds(start, tq)]
            if masked:
                t = jnp.where(col <= row, t, -jnp.inf)
            m_new = jnp.maximum(m, jnp.max(t, axis=-1, keepdims=True) + fq)
            p = jnp.exp(t + (fq - m_new))
            alpha = jnp.exp(m - m_new)
            l = alpha * l + jnp.sum(p, axis=-1, keepdims=True)
            acc = alpha * acc + jnp.dot(p.astype(BF16), vb, preferred_element_type=F32)
            out.append((m_new, l, acc))
        return tuple(out)

    init1 = (jnp.full((tq, 1), -jnp.inf, F32), jnp.zeros((tq, 1), F32), jnp.zeros((tq, LANES), F32))
    carry = lax.fori_loop(0, i, functools.partial(kv_step, masked=False), (init1, init1))
    (_, l0, acc0), (_, l1, acc1) = kv_step(i, carry, True)
    o_ref[...] = jnp.where(first, acc0 / l0, acc1 / l1).astype(o_ref.dtype)


def fox_attention(q, k, v, fh, fht, batch, seq, *, tq=1024):
    t = q.shape[0]
    nq = seq // tq
    pairs = FOX_HEADS // 2
    fh4 = fh.reshape(batch, pairs, 2, seq)
    return pl.pallas_call(
        functools.partial(_fox_attn_kernel, tq=tq),
        grid=(batch, pairs, nq),
        in_specs=[pl.BlockSpec((tq, LANES), lambda b, p, i: (b * nq + i, p)),
                  pl.BlockSpec((seq, LANES), lambda b, p, i: (b, p)),
                  pl.BlockSpec((seq, LANES), lambda b, p, i: (b, p)),
                  pl.BlockSpec((None, None, 2, seq), lambda b, p, i: (b, p, 0, 0)),
                  pl.BlockSpec((tq, FOX_HEADS), lambda b, p, i: (b * nq + i, 0))],
        out_specs=pl.BlockSpec((tq, LANES), lambda b, p, i: (b * nq + i, p)),
        out_shape=jax.ShapeDtypeStruct((t, FOX_DIM), BF16),
        compiler_params=_params("parallel", "parallel", "arbitrary"),
        name="fox_attention",
    )(q, k, v, fh4, fht)


def _out_proj2_kernel(h_ref, a_ref, b_ref, wa_ref, wb_ref, o_ref):
    o_ref[...] = (h_ref[...]
                  + jnp.dot(a_ref[...], wa_ref[...], preferred_element_type=F32)
                  + jnp.dot(b_ref[...], wb_ref[...], preferred_element_type=F32))


def out_proj2(h, a, b, wa, wb, *, tm=1024):
    t, d = h.shape
    ka, kb = a.shape[1], b.shape[1]
    return pl.pallas_call(
        _out_proj2_kernel,
        grid=(t // tm,),
        in_specs=[pl.BlockSpec((tm, d), lambda i: (i, 0)),
                  pl.BlockSpec((tm, ka), lambda i: (i, 0)),
                  pl.BlockSpec((tm, kb), lambda i: (i, 0)),
                  pl.BlockSpec((ka, d), lambda i: (0, 0)),
                  pl.BlockSpec((kb, d), lambda i: (0, 0))],
        out_specs=pl.BlockSpec((tm, d), lambda i: (i, 0)),
        out_shape=jax.ShapeDtypeStruct((t, d), F32),
        compiler_params=_params("parallel"),
        name="out_proj2",
    )(h, a, b, wa, wb)


def _out_proj1_kernel(h_ref, a_ref, w_ref, o_ref):
    o_ref[...] = h_ref[...] + jnp.dot(a_ref[...], w_ref[...], preferred_element_type=F32)


def out_proj1(h, a, w, *, tm=1024):
    t, d = h.shape
    ka = a.shape[1]
    return pl.pallas_call(
        _out_proj1_kernel,
        grid=(t // tm,),
        in_specs=[pl.BlockSpec((tm, d), lambda i: (i, 0)),
                  pl.BlockSpec((tm, ka), lambda i: (i, 0)),
                  pl.BlockSpec((ka, d), lambda i: (0, 0))],
        out_specs=pl.BlockSpec((tm, d), lambda i: (i, 0)),
        out_shape=jax.ShapeDtypeStruct((t, d), F32),
        compiler_params=_params("parallel"),
        name="out_proj1",
    )(h, a, w)


def _shift_in_group(x, s, fill):
    rows = lax.broadcasted_iota(I32, x.shape, 0) % SUBLANES
    return jnp.where(rows < s, fill, pltpu.roll(x, s, 0))


def _lru_kernel(gate_ref, xr_ref, cw_ref, cb_ref, wa_ref, ba_ref, wx_ref, bx_ref, lam_ref,
                o_ref, halo_ref, carry_ref, *, ts):
    @pl.when(pl.program_id(1) == 0)
    def _():
        halo_ref[...] = jnp.zeros_like(halo_ref)
        carry_ref[...] = jnp.zeros_like(carry_ref)

    x = xr_ref[...]
    ext = jnp.concatenate([halo_ref[...], x], axis=0)
    halo_ref[...] = x[ts - SUBLANES:, :]
    cw = cw_ref[...]
    xc = cb_ref[...] + cw[3:4] * x
    for k in range(LRU_CONV_W - 1):
        xc = xc + cw[k:k + 1] * pltpu.roll(ext, LRU_CONV_W - 1 - k, 0)[SUBLANES:]

    xcb = xc.astype(BF16)
    ra, ri = [], []
    for h in range(LRU_HEADS):
        xh = xcb[:, h * LRU_BLOCK:(h + 1) * LRU_BLOCK]
        ra.append(jnp.dot(xh, wa_ref[h], preferred_element_type=F32))
        ri.append(jnp.dot(xh, wx_ref[h], preferred_element_type=F32))
    r = jax.nn.sigmoid(jnp.concatenate(ra, axis=1) + ba_ref[...])
    gi = jax.nn.sigmoid(jnp.concatenate(ri, axis=1) + bx_ref[...])
    nl = -lam_ref[...]
    softplus = jnp.maximum(nl, 0.0) + jnp.log1p(jnp.exp(-jnp.abs(nl)))
    log_a = -LRU_C * r * softplus
    a = jnp.exp(log_a)
    mult = jnp.sqrt(-jnp.tanh(log_a) * (a * a + 1.0))
    u = mult * (gi * xc)

    s = 1
    while s < SUBLANES:
        u = a * _shift_in_group(u, s, 0.0) + u
        a = a * _shift_in_group(a, s, 1.0)
        s *= 2
    state = carry_ref[0:1, :]
    groups = []
    for g in range(ts // SUBLANES):
        rows = slice(g * SUBLANES, (g + 1) * SUBLANES)
        hg = u[rows] + a[rows] * state
        state = hg[SUBLANES - 1:SUBLANES, :]
        groups.append(hg)
    carry_ref[...] = jnp.broadcast_to(state, carry_ref.shape)
    hs = jnp.concatenate(groups, axis=0)
    o_ref[...] = (hs * _gelu_tanh(gate_ref[...])).astype(o_ref.dtype)


def lru_mix(z, conv_w, conv_b, w_a, b_a, w_x, b_x, lam, batch, seq, *, ts=256):
    t = z.shape[0]
    w = z.shape[1] // 2
    nb = seq // ts
    row = lambda: pl.BlockSpec((1, w), lambda b, i: (0, 0))
    blockw = lambda: pl.BlockSpec((LRU_HEADS, LRU_BLOCK, LRU_BLOCK), lambda b, i: (0, 0, 0))
    return pl.pallas_call(
        functools.partial(_lru_kernel, ts=ts),
        grid=(batch, nb),
        in_specs=[pl.BlockSpec((ts, w), lambda b, i: (b * nb + i, 0)),
                  pl.BlockSpec((ts, w), lambda b, i: (b * nb + i, 1)),
                  pl.BlockSpec((LRU_CONV_W, w), lambda b, i: (0, 0)),
                  row(), blockw(), row(), blockw(), row(), row()],
        out_specs=pl.BlockSpec((ts, w), lambda b, i: (b * nb + i, 0)),
        out_shape=jax.ShapeDtypeStruct((t, w), BF16),
        scratch_shapes=[pltpu.VMEM((SUBLANES, w), F32), pltpu.VMEM((SUBLANES, w), F32)],
        compiler_params=_params("parallel", "arbitrary"),
        name="lru_mix",
    )(z, z, conv_w, conv_b.reshape(1, w), w_a.astype(BF16), b_a.reshape(1, w),
      w_x.astype(BF16), b_x.reshape(1, w), lam.reshape(1, w))


def _top_rows(vals, order, payload, k):
    nlane = vals.shape[1]
    kio = lax.broadcasted_iota(I32, (k, nlane), 0)
    top_v = jnp.zeros((k, nlane), F32)
    top_p = jnp.zeros((k, nlane), F32)
    for r in range(k):
        m = jnp.max(vals, axis=0, keepdims=True)
        first = jnp.min(jnp.where(vals == m, order, jnp.inf), axis=0, keepdims=True)
        sel = order == first
        if payload is None:
            p = first
        else:
            p = jnp.sum(jnp.where(sel, payload, 0.0), axis=0, keepdims=True)
        vals = jnp.where(sel, -jnp.inf, vals)
        top_v = jnp.where(kio == r, m, top_v)
        top_p = jnp.where(kio == r, p, top_p)
    return top_v, top_p


def _pair_candidates(s1, i1, s2, i2):
    k = PEER_TOPK
    half = k // 2
    sub = lax.broadcasted_iota(I32, (half, s1.shape[1]), 0).astype(F32)
    vals, flat, expert = [], [], []
    for a in range(half):
        vals.append(s1[a:a + 1, :] + s2[:half])
        flat.append(sub + float(a * k))
        expert.append(i1[a:a + 1, :] * float(N_KEYS) + i2[:half])
    vals.append(s1[0:1, :] + s2[half:])
    flat.append(sub + float(half))
    expert.append(i1[0:1, :] * float(N_KEYS) + i2[half:])
    vals.append(s1[half:] + s2[0:1, :])
    flat.append((sub + float(half)) * float(k))
    expert.append(i1[half:] * float(N_KEYS) + i2[0:1, :])
    return (jnp.concatenate(vals, axis=0), jnp.concatenate(flat, axis=0),
            jnp.concatenate(expert, axis=0))


def _peer_route_kernel(h_ref, g_ref, wq_ref, keys_ref, xn_ref, idx_ref, gate_ref):
    xn = _rms(h_ref[...], g_ref[...]).astype(BF16)
    xn_ref[...] = xn
    q = jnp.dot(xn, wq_ref[...], preferred_element_type=F32).astype(BF16)
    k = PEER_TOPK
    key_id = lax.broadcasted_iota(I32, (N_KEYS, q.shape[0]), 0).astype(F32)
    idx_rows, gate_rows = [], []
    for h in range(PEER_HEADS):
        tops = []
        for c in range(2):
            qhc = q[:, (2 * h + c) * HALF:(2 * h + c + 1) * HALF]
            sc = lax.dot_general(keys_ref[c], qhc, (((1,), (1,)), ((), ())), preferred_element_type=F32)
            tops.append(_top_rows(sc, key_id, None, k))
        (s1, i1), (s2, i2) = tops
        cand, flat, expert = _pair_candidates(s1, i1, s2, i2)
        best_s, best_e = _top_rows(cand, flat, expert, k)
        e = jnp.exp(best_s - jnp.max(best_s, axis=0, keepdims=True))
        gate_rows.append(e / jnp.sum(e, axis=0, keepdims=True))
        idx_rows.append(best_e)
    idx_t = jnp.concatenate(idx_rows, axis=0)
    gate_t = jnp.concatenate(gate_rows, axis=0)
    idx_ref[...] = idx_t.T.astype(I32)
    gate_ref[...] = gate_t.T


def peer_route(h, gain, wq, keys):
    t, d = h.shape
    tb = ROUTE_TOKENS
    nq = wq.shape[1]
    return pl.pallas_call(
        _peer_route_kernel,
        grid=(t // tb,),
        in_specs=[pl.BlockSpec((tb, d), lambda i: (i, 0)),
                  pl.BlockSpec((1, d), lambda i: (0, 0)),
                  pl.BlockSpec((d, nq), lambda i: (0, 0)),
                  pl.BlockSpec((2, N_KEYS, HALF), lambda i: (0, 0, 0))],
        out_specs=[pl.BlockSpec((tb, d), lambda i: (i, 0)),
                   pl.BlockSpec((tb, PEER_HEADS * PEER_TOPK), lambda i: (i, 0)),
                   pl.BlockSpec((tb, PEER_HEADS * PEER_TOPK), lambda i: (i, 0))],
        out_shape=[jax.ShapeDtypeStruct((t, d), BF16),
                   jax.ShapeDtypeStruct((t, PEER_HEADS * PEER_TOPK), I32),
                   jax.ShapeDtypeStruct((t, PEER_HEADS * PEER_TOPK), F32)],
        compiler_params=_params("parallel"),
        name="peer_route",
    )(h, gain.reshape(1, d), wq, keys)


def _expert_tokens(chunk_words, start_copies, xn_ref, gate_ref, h_ref, o_ref, vstash, *, tb, d):
    nsel = PEER_HEADS * PEER_TOPK
    nchunk = d // LANES
    acts, ws = {}, {}
    for t in range(tb + 2):
        if t < tb:
            x8 = jnp.broadcast_to(xn_ref[t:t + 1, :], (SUBLANES, d))
            act = jnp.zeros((SUBLANES, 2 * nsel), F32)
        for c in range(nchunk):
            if t < tb:
                if start_copies is not None:
                    start_copies(t, c)
                z = pltpu.bitcast(chunk_words(t, c), BF16)
                vstash[t % 3, c] = z
                act = act + lax.dot_general(x8[:, c * LANES:(c + 1) * LANES], z, (((1,), (1,)), ((), ())),
                                            preferred_element_type=F32)
            if t >= 2:
                out = jnp.dot(ws[t - 2], vstash[(t - 2) % 3, c], preferred_element_type=F32)
                cols = slice(c * LANES, (c + 1) * LANES)
                o_ref[t - 2:t - 1, cols] = h_ref[t - 2:t - 1, cols] + out[0:1, :]
        if t < tb:
            acts[t] = act
        if 1 <= t <= tb:
            w2 = gate_ref[t - 1:t, :] * _gelu_tanh(acts.pop(t - 1))
            ws[t - 1] = pltpu.roll(w2, 2 * nsel - 1, 1).astype(BF16)


def _peer_expert_kernel(idx0, idx1, idx2, xn_ref, gate_ref, h_ref, tab_ref, o_ref, buf, vstash, sem,
                        *, tb, nblk, d):
    i = pl.program_id(0)
    slot = i % GATHER_SLOTS
    ahead = (i + 2) % GATHER_SLOTS
    nsel = PEER_HEADS * PEER_TOPK
    tile = SUBLANES
    nchunk = d // LANES
    per_tok = nsel * tile
    per_chunk = nsel // nchunk

    def tile_copy(e, s, r):
        return pltpu.make_async_copy(
            tab_ref.at[pl.ds(pl.multiple_of(e * tile, tile), tile)],
            buf.at[s, pl.ds(pl.multiple_of(r * tile, tile), tile)], sem.at[s])

    def slot_wait(s):
        pltpu.make_async_copy(tab_ref.at[pl.ds(0, tb * per_tok)], buf.at[s], sem.at[s]).wait()

    @pl.when(i == 0)
    def _():
        def prime(r, c):
            tile_copy(idx0[r], 0, r).start()
            tile_copy(idx1[r], 1, r).start()
            return c
        lax.fori_loop(0, tb * nsel, prime, 0)

    slot_wait(slot)

    def chunk_words(t, c):
        return buf[slot, pl.ds(t * per_tok + c, nsel, stride=tile), :]

    def start_copies(t, c):
        for n in range(c * per_chunk, (c + 1) * per_chunk):
            tile_copy(idx2[t * nsel + n], ahead, t * nsel + n).start(priority=n % 2)

    _expert_tokens(chunk_words, start_copies, xn_ref, gate_ref, h_ref, o_ref, vstash, tb=tb, d=d)

    @pl.when(i == nblk - 1)
    def _():
        slot_wait(nblk % GATHER_SLOTS)
        slot_wait((nblk + 1) % GATHER_SLOTS)


def pack_expert_table(u, v):
    ne, d = u.shape
    pairs = jnp.stack([v.astype(BF16), u.astype(BF16)], axis=-1)
    words = lax.bitcast_convert_type(pairs, I32)
    return words.reshape(ne * (d // LANES), LANES)


def peer_expert(h, xn, idx, gate, table):
    t, d = h.shape
    assert d == SUBLANES * LANES
    tb = EXPERT_TOKENS
    nsel = PEER_HEADS * PEER_TOPK
    nblk = t // tb
    assert nblk >= 2
    idx_flat = idx.reshape(t * nsel)
    smem = lambda k: pl.BlockSpec((tb * nsel,), lambda i: (jnp.minimum(i + k, nblk - 1),),
                                  memory_space=pltpu.SMEM)
    return pl.pallas_call(
        functools.partial(_peer_expert_kernel, tb=tb, nblk=nblk, d=d),
        grid=(nblk,),
        in_specs=[smem(0), smem(1), smem(2),
                  pl.BlockSpec((tb, d), lambda i: (i, 0)),
                  pl.BlockSpec((tb, 2 * nsel), lambda i: (i, 0)),
                  pl.BlockSpec((tb, d), lambda i: (i, 0)),
                  pl.BlockSpec(memory_space=pl.ANY)],
        out_specs=pl.BlockSpec((tb, d), lambda i: (i, 0)),
        out_shape=jax.ShapeDtypeStruct((t, d), F32),
        scratch_shapes=[pltpu.VMEM((GATHER_SLOTS, tb * nsel * SUBLANES, LANES), I32),
                        pltpu.VMEM((3, d // LANES, 2 * nsel, LANES), BF16),
                        pltpu.SemaphoreType.DMA((GATHER_SLOTS,))],
        compiler_params=_params("arbitrary"),
        name="peer_expert",
    )(idx_flat, idx_flat, idx_flat, xn,
      jnp.stack([jnp.zeros_like(gate), gate], axis=-1).reshape(t, 2 * nsel), h, table)


def _ple_kernel(h_ref, g_ref, p_ref, wg_ref, wu_ref, o_ref):
    h = h_ref[...]
    xn = _rms(h, g_ref[...]).astype(BF16)
    gate = jax.nn.sigmoid(jnp.dot(xn, wg_ref[...], preferred_element_type=F32))
    up = jnp.dot(p_ref[...].astype(BF16), wu_ref[...], preferred_element_type=F32)
    o_ref[...] = h + up * gate


def ple(h, gain, p, w_gate, w_up, *, tm=1024):
    t, d = h.shape
    pd = p.shape[1]
    return pl.pallas_call(
        _ple_kernel,
        grid=(t // tm,),
        in_specs=[pl.BlockSpec((tm, d), lambda i: (i, 0)),
                  pl.BlockSpec((1, d), lambda i: (0, 0)),
                  pl.BlockSpec((tm, pd), lambda i: (i, 0)),
                  pl.BlockSpec((d, d), lambda i: (0, 0)),
                  pl.BlockSpec((pd, d), lambda i: (0, 0))],
        out_specs=pl.BlockSpec((tm, d), lambda i: (i, 0)),
        out_shape=jax.ShapeDtypeStruct((t, d), F32),
        compiler_params=_params("parallel"),
        name="ple",
    )(h, gain.reshape(1, d), p, w_gate, w_up)


def _mix_even(h, gain, w_in, b_forget, conv_w, q_gain, k_gain, w_out, batch, seq):
    main = 3 * CONV_CH + 3 * FOX_DIM
    w_main = w_in[:, :main].astype(BF16)
    w_f = w_in[:, main:].T.astype(BF16)
    z, f_t = norm_matmul(h, gain, w_main, w_f)
    ya = conv_even(z, conv_w, seq)
    q, k, v = qkv_prep(z, q_gain, k_gain)
    fh, fht = forget_cumsum(f_t, b_forget, batch, seq)
    yb = fox_attention(q, k, v, fh, fht, batch, seq)
    wo = w_out.astype(BF16)
    return out_proj2(h, ya, yb, wo[:CONV_CH], wo[CONV_CH:])


def _mix_odd(h, gain, w_in, conv_w, conv_b, w_a, b_a, w_x, b_x, lam, w_out, batch, seq):
    z = norm_matmul(h, gain, w_in.astype(BF16))
    y = lru_mix(z, conv_w, conv_b, w_a, b_a, w_x, b_x, lam, batch, seq)
    return out_proj1(h, y, w_out.astype(BF16))


def _peer(h, gain, w_query, sub_keys, u, v):
    xn, idx, gate = peer_route(h, gain, w_query.astype(BF16), sub_keys.astype(BF16))
    return peer_expert(h, xn, idx, gate, pack_expert_table(u, v))


def kernel(x, p, norm_mix, norm_ffn, norm_ple, even_w_in, even_b_forget, even_conv_w, even_q_gain,
           even_k_gain, even_w_out, odd_w_in, odd_conv_w, odd_conv_b, odd_w_a, odd_b_a, odd_w_x,
           odd_b_x, odd_lru_param, odd_w_out, peer_w_query, peer_sub_keys, peer_u, peer_v,
           ple_w_up, ple_w_gate):
    batch, seq, d = x.shape
    depth = p.shape[0]
    h = x.reshape(batch * seq, d)
    for layer in range(depth):
        j = layer // 2
        if layer % 2 == 0:
            h = _mix_even(h, norm_mix[layer], even_w_in[j], even_b_forget[j], even_conv_w[j],
                          even_q_gain[j], even_k_gain[j], even_w_out[j], batch, seq)
        else:
            h = _mix_odd(h, norm_mix[layer], odd_w_in[j], odd_conv_w[j], odd_conv_b[j], odd_w_a[j],
                         odd_b_a[j], odd_w_x[j], odd_b_x[j], odd_lru_param[j], odd_w_out[j], batch, seq)
        h = _peer(h, norm_ffn[layer], peer_w_query[layer], peer_sub_keys[layer], peer_u[layer],
                  peer_v[layer])
        h = ple(h, norm_ple[layer], p[layer].reshape(batch * seq, -1), ple_w_gate[layer].astype(BF16),
                ple_w_up[layer].astype(BF16))
    return h.reshape(batch, seq, d)
```
